```python
import math
import jax, jax.numpy as jnp
from jax import lax
import numpy as np

D_MODEL = 1024
BATCH = 2
SEQ = 8192
DEPTH = 4
DEC_BATCH = 128
DEC_SEQ = 1
PAST_LEN = 2048
PAGE_SIZE = 128

N_EVEN = (DEPTH + 1) // 2
N_ODD = DEPTH // 2
A_WIDTH = D_MODEL // 2
A_GROUPS = 4
A_GROUP_DIM = A_WIDTH // A_GROUPS
CHUNK = 128
B_WIDTH = D_MODEL // 2
B_HS = 64
B_HEADS = B_WIDTH // B_HS
W_LORA = 64
A_LORA = 64
B_SHIFT = 3 * B_WIDTH + W_LORA + A_LORA
EVEN_COLS = 3 * A_WIDTH + B_SHIFT + B_WIDTH
C_HD = 64
C_VD = 2 * C_HD
C_HEADS = D_MODEL // C_VD
C_WIDTH = C_HEADS * C_VD
ROT_DIM = C_HD // 4
ROPE_THETA = 500000.0
Q_BLOCK = 128
ODD_COLS = 2 * C_HEADS * 2 * C_HD + 2 * C_WIDTH
EPS = 1e-6
GN_EPS = 64e-5

kernel_name = "hybrid_gmlp_rwkv7_diffattn_step"


def rmsnorm(x, g):
    x32 = x.astype(jnp.float32)
    y = x32 * lax.rsqrt(jnp.mean(x32 * x32, axis=-1, keepdims=True) + EPS)
    return (y * g.astype(jnp.float32)).astype(x.dtype)


def layernorm(x, g, b, eps):
    x32 = x.astype(jnp.float32)
    xc = x32 - jnp.mean(x32, axis=-1, keepdims=True)
    y = xc * lax.rsqrt(jnp.mean(xc * xc, axis=-1, keepdims=True) + eps)
    return (y * g.astype(jnp.float32) + b.astype(jnp.float32)).astype(x.dtype)


def modulate(x, c, norm_g, ada_w, ada_b):
    mod = jax.nn.silu(c) @ ada_w + ada_b
    shift, scale, gate = jnp.split(mod, 3, axis=-1)
    h = rmsnorm(x, norm_g) * (1 + scale[:, None]) + shift[:, None]
    return h, gate[:, None]


def chunk_spatial_mix(v, w_s, b_s):
    bsz, t = v.shape[0], v.shape[1]
    n_chunks = -(-t // CHUNK)
    pad = n_chunks * CHUNK - t
    vp = jnp.pad(v, ((0, 0), (0, pad), (0, 0), (0, 0))).reshape(bsz, n_chunks, CHUNK, A_GROUPS, A_GROUP_DIM)
    mask = jnp.tril(jnp.ones((CHUNK, CHUNK), dtype=bool))
    w = jnp.where(mask, w_s, 0).astype(v.dtype)
    out = jnp.einsum('gij,bnjgc->bnigc', w, vp) + b_s.T[:, :, None].astype(v.dtype)
    return out.reshape(bsz, n_chunks * CHUNK, A_GROUPS, A_GROUP_DIM)[:, :t]


def rwkv7_scan(r, decay, k, v, kk, a, s0):
    def step(s, inp):
        r_t, d_t, k_t, v_t, kk_t, a_t = inp
        sa = jnp.einsum('bhvk,bhk->bhv', s, -kk_t)
        s = (s * d_t[:, :, None, :] + sa[..., None] * (kk_t * a_t)[:, :, None, :]
             + v_t[..., None] * k_t[:, :, None, :])
        return s, jnp.einsum('bhvk,bhk->bhv', s, r_t)
    xs = tuple(jnp.swapaxes(z, 0, 1) for z in (r, decay, k, v, kk, a))
    s_final, ys = lax.scan(step, s0, xs)
    return jnp.swapaxes(ys, 0, 1), s_final


def even_layer(h, shift_prev, wkv0, w_in, w_out, a_vn_g, a_vn_b, a_ws, a_bs, b_mu, b_w0, b_w2, b_a0, b_a2,
               b_kk, b_ka, b_rk, b_gn_g, b_gn_b):
    bsz, t, _ = h.shape
    f32 = jnp.float32
    p = h @ w_in
    a_u, a_v, a_z, pb, b_z = jnp.split(p, [A_WIDTH, 2 * A_WIDTH, 3 * A_WIDTH, 3 * A_WIDTH + B_SHIFT], axis=-1)
    u = jax.nn.gelu(a_u)
    vg = jax.nn.gelu(a_v).reshape(bsz, t, A_GROUPS, A_GROUP_DIM)
    vn = layernorm(vg, a_vn_g.reshape(A_GROUPS, A_GROUP_DIM), a_vn_b.reshape(A_GROUPS, A_GROUP_DIM), 1e-5)
    out_a = u * chunk_spatial_mix(vn, a_ws, a_bs).reshape(bsz, t, A_WIDTH) * jax.nn.silu(a_z)
    prev = jnp.concatenate([shift_prev[:, None].astype(pb.dtype), pb[:, :-1]], axis=1)
    xs = pb + b_mu * (prev - pb)
    r, k, v, wlo, alo = jnp.split(xs, [B_WIDTH, 2 * B_WIDTH, 3 * B_WIDTH, 3 * B_WIDTH + W_LORA], axis=-1)
    w = -jax.nn.softplus(-(b_w0 + jnp.tanh(wlo) @ b_w2)) - 0.5
    decay = jnp.exp(-jnp.exp(w.astype(f32)))
    a = jax.nn.sigmoid(b_a0 + alo @ b_a2)
    hd = lambda z: z.reshape(bsz, t, B_HEADS, B_HS).astype(f32)
    kk = hd(k * b_kk)
    kk = kk / jnp.maximum(jnp.sqrt(jnp.sum(kk * kk, axis=-1, keepdims=True)), 1e-12)
    k_mod = hd(k * (1 + (a - 1) * b_ka))
    r_h, v_h, a_h = hd(r), hd(v), hd(a)
    y, s_new = rwkv7_scan(r_h, hd(decay), k_mod, v_h, kk, a_h, wkv0.astype(f32))
    y = layernorm(y, b_gn_g.reshape(B_HEADS, B_HS), b_gn_b.reshape(B_HEADS, B_HS), GN_EPS)
    y = y + jnp.sum(r_h * k_mod * b_rk.astype(f32), axis=-1, keepdims=True) * v_h
    out_b = y.reshape(bsz, t, B_WIDTH).astype(h.dtype) * jax.nn.silu(b_z)
    out = jnp.concatenate([out_a, out_b], axis=-1) @ w_out
    return out, vn.reshape(bsz, t, A_WIDTH), pb[:, -1], s_new


def rope_partial(x, pos):
    half = ROT_DIM // 2
    inv = ROPE_THETA ** (-jnp.arange(half, dtype=jnp.float32) / half)
    ang = pos.astype(jnp.float32)[:, None] * inv[None]
    cos = jnp.cos(ang)[None, :, None, None, :]
    sin = jnp.sin(ang)[None, :, None, None, :]
    xr = x[..., :ROT_DIM].astype(jnp.float32)
    x1, x2 = xr[..., :half], xr[..., half:]
    rot = jnp.concatenate([x1 * cos - x2 * sin, x2 * cos + x1 * sin], axis=-1).astype(x.dtype)
    return jnp.concatenate([rot, x[..., ROT_DIM:]], axis=-1)


def odd_project(h, pos, w_in):
    bsz, t, _ = h.shape
    qk = C_HEADS * 2 * C_HD
    q, k, v, z = jnp.split(h @ w_in, [qk, 2 * qk, 2 * qk + C_WIDTH], axis=-1)
    q = rope_partial(q.reshape(bsz, t, C_HEADS, 2, C_HD), pos)
    k = rope_partial(k.reshape(bsz, t, C_HEADS, 2, C_HD), pos)
    return q, k, v.reshape(bsz, t, C_HEADS, C_VD), z


def diff_lambda(lq1, lk1, lq2, lk2, lam_init):
    f32 = jnp.float32
    return (jnp.exp(jnp.sum(lq1.astype(f32) * lk1.astype(f32))) - jnp.exp(jnp.sum(lq2.astype(f32) * lk2.astype(f32)))
            + lam_init)


def diff_attn_prompt(q, k, v, lam):
    bsz, t = q.shape[0], q.shape[1]
    nb = t // Q_BLOCK
    qb = jnp.moveaxis(q.reshape(bsz, nb, Q_BLOCK, C_HEADS, 2, C_HD), 1, 0)
    kpos = jnp.arange(t)
    scale = C_HD ** -0.5

    def block(args):
        q_blk, i = args
        s = jnp.einsum('bqhsd,bkhsd->bhsqk', q_blk, k).astype(jnp.float32) * scale
        qpos = i * Q_BLOCK + jnp.arange(Q_BLOCK)
        s = jnp.where(kpos[None, :] <= qpos[:, None], s, -jnp.inf)
        p = jax.nn.softmax(s, axis=-1)
        attn = p[:, :, 0] - lam * p[:, :, 1]
        return jnp.einsum('bhqk,bkhe->bqhe', attn.astype(v.dtype), v)

    out = lax.map(block, (qb, jnp.arange(nb)))
    return jnp.moveaxis(out, 0, 1).reshape(bsz, t, C_HEADS, C_VD)


def diff_attn_sample(q, k_new, v_new, k_past, v_past, lam):
    scale = C_HD ** -0.5
    s_len = q.shape[1]
    past = k_past.shape[1]
    s_past = jnp.einsum('bqhsd,bkhsd->bhsqk', q, k_past).astype(jnp.float32) * scale
    s_new = jnp.einsum('bqhsd,bkhsd->bhsqk', q, k_new).astype(jnp.float32) * scale
    s_new = jnp.where(jnp.tril(jnp.ones((s_len, s_len), dtype=bool)), s_new, -jnp.inf)
    p = jax.nn.softmax(jnp.concatenate([s_past, s_new], axis=-1), axis=-1)
    attn = (p[:, :, 0] - lam * p[:, :, 1]).astype(v_new.dtype)
    return (jnp.einsum('bhqk,bkhe->bqhe', attn[..., :past], v_past)
            + jnp.einsum('bhqk,bkhe->bqhe', attn[..., past:], v_new))


def odd_finish(o, z, lam_init, subln_g, w_out):
    bsz, t = o.shape[0], o.shape[1]
    o = rmsnorm(o, subln_g) * (1 - lam_init)
    return (o.reshape(bsz, t, C_WIDTH) * jax.nn.silu(z)) @ w_out


def setup_inputs(seed: int = 0) -> dict:
    key = jax.random.key(seed)
    ks = iter(jax.random.split(key, 48))
    nrm = lambda shape, s=1.0: jax.random.normal(next(ks), shape, jnp.float32) * s
    n_pages = PAST_LEN // PAGE_SIZE
    n_used = DEC_BATCH * n_pages
    n_pool = n_used + n_used // 4
    page_table = jax.random.permutation(next(ks), n_pool)[:n_used].reshape(DEC_BATCH, n_pages).astype(jnp.int32)
    d = D_MODEL
    return {
        "x_prompt": nrm((BATCH, SEQ, d)),
        "x_sample": nrm((DEC_BATCH, DEC_SEQ, d)),
        "c_prompt": nrm((BATCH, d)),
        "c_sample": nrm((DEC_BATCH, d)),
        "state_wkv": nrm((N_EVEN, DEC_BATCH, B_HEADS, B_HS, B_HS)),
        "state_shift": nrm((N_EVEN, DEC_BATCH, B_SHIFT)),
        "cache_k": nrm((N_ODD, n_pool, PAGE_SIZE, C_HEADS, 2 * C_HD)),
        "cache_v": nrm((N_ODD, n_pool, PAGE_SIZE, C_HEADS, C_VD)),
        "page_table": page_table,
        "ada_w": nrm((DEPTH, d, 3 * d), d ** -0.5),
        "ada_b": nrm((DEPTH, 3 * d), 0.02),
        "norm_g": 1.0 + nrm((DEPTH, d), 0.02),
        "final_g": 1.0 + nrm((d,), 0.02),
        "even_w_in": nrm((N_EVEN, d, EVEN_COLS), d ** -0.5),
        "even_w_out": nrm((N_EVEN, A_WIDTH + B_WIDTH, d), (A_WIDTH + B_WIDTH) ** -0.5),
        "a_vn_g": 1.0 + nrm((N_EVEN, A_WIDTH), 0.02),
        "a_vn_b": nrm((N_EVEN, A_WIDTH), 0.02),
        "a_ws": nrm((N_EVEN, A_GROUPS, CHUNK, CHUNK), CHUNK ** -0.5),
        "a_bs": 1.0 + nrm((N_EVEN, A_GROUPS, CHUNK), 0.1),
        "b_mu": jax.random.uniform(next(ks), (N_EVEN, B_SHIFT), jnp.float32),
        "b_w0": jax.random.uniform(next(ks), (N_EVEN, B_WIDTH), jnp.float32, -2.0, 1.0),
        "b_w2": nrm((N_EVEN, W_LORA, B_WIDTH), 0.5 * W_LORA ** -0.5),
        "b_a0": nrm((N_EVEN, B_WIDTH), 0.1),
        "b_a2": nrm((N_EVEN, A_LORA, B_WIDTH), 0.5 * A_LORA ** -0.5),
        "b_kk": 1.0 + nrm((N_EVEN, B_WIDTH), 0.1),
        "b_ka": 1.0 + nrm((N_EVEN, B_WIDTH), 0.1),
        "b_rk": nrm((N_EVEN, B_HEADS, B_HS), 0.1),
        "b_gn_g": 1.0 + nrm((N_EVEN, B_WIDTH), 0.02),
        "b_gn_b": nrm((N_EVEN, B_WIDTH), 0.02),
        "odd_w_in": nrm((N_ODD, d, ODD_COLS), d ** -0.5),
        "odd_w_out": nrm((N_ODD, C_WIDTH, d), C_WIDTH ** -0.5),
        "c_lq1": nrm((N_ODD, C_HD), 0.1),
        "c_lk1": nrm((N_ODD, C_HD), 0.1),
        "c_lq2": nrm((N_ODD, C_HD), 0.1),
        "c_lk2": nrm((N_ODD, C_HD), 0.1),
        "c_subln_g": 1.0 + nrm((N_ODD, C_VD), 0.02),
    }


def reference(x_prompt, x_sample, c_prompt, c_sample, state_wkv, state_shift, cache_k, cache_v, page_table,
              ada_w, ada_b, norm_g, final_g, even_w_in, even_w_out, a_vn_g, a_vn_b, a_ws, a_bs, b_mu, b_w0, b_w2,
              b_a0, b_a2, b_kk, b_ka, b_rk, b_gn_g, b_gn_b, odd_w_in, odd_w_out, c_lq1, c_lk1, c_lq2, c_lk2,
              c_subln_g):
    xp, xs = x_prompt, x_sample
    bsz_p, t_p = xp.shape[0], xp.shape[1]
    bsz_s, t_s = xs.shape[0], xs.shape[1]
    past = page_table.shape[1] * PAGE_SIZE
    pos_p = jnp.arange(t_p)
    pos_s = past + jnp.arange(t_s)
    wkv_p, shift_p, k_p, v_p = [], [], [], []
    wkv_s, shift_s, av_s, k_s, v_s = [], [], [], [], []
    for l in range(DEPTH):
        hp, gp = modulate(xp, c_prompt, norm_g[l], ada_w[l], ada_b[l])
        hs, gs = modulate(xs, c_sample, norm_g[l], ada_w[l], ada_b[l])
        if l % 2 == 0:
            e = l // 2
            ew = (even_w_in[e], even_w_out[e], a_vn_g[e], a_vn_b[e], a_ws[e], a_bs[e], b_mu[e], b_w0[e], b_w2[e],
                  b_a0[e], b_a2[e], b_kk[e], b_ka[e], b_rk[e], b_gn_g[e], b_gn_b[e])
            zero_shift = jnp.zeros((bsz_p, B_SHIFT), xp.dtype)
            zero_wkv = jnp.zeros((bsz_p, B_HEADS, B_HS, B_HS), jnp.float32)
            op, _, shp, sp = even_layer(hp, zero_shift, zero_wkv, *ew)
            os_, avs, shs, ss = even_layer(hs, state_shift[e], state_wkv[e], *ew)
            wkv_p.append(sp); shift_p.append(shp)
            wkv_s.append(ss); shift_s.append(shs); av_s.append(avs)
        else:
            o = l // 2
            lam_init = 0.8 - 0.6 * math.exp(-0.3 * l)
            lam = diff_lambda(c_lq1[o], c_lk1[o], c_lq2[o], c_lk2[o], lam_init)
            qp, kp, vp, zp = odd_project(hp, pos_p, odd_w_in[o])
            op = odd_finish(diff_attn_prompt(qp, kp, vp, lam), zp, lam_init, c_subln_g[o], odd_w_out[o])
            qs, kn, vn, zs = odd_project(hs, pos_s, odd_w_in[o])
            k_past = cache_k[o][page_table].reshape(bsz_s, past, C_HEADS, 2, C_HD)
            v_past = cache_v[o][page_table].reshape(bsz_s, past, C_HEADS, C_VD)
            os_ = odd_finish(diff_attn_sample(qs, kn, vn, k_past, v_past, lam), zs, lam_init, c_subln_g[o],
                             odd_w_out[o])
            k_p.append(kp.reshape(bsz_p, t_p, C_HEADS, 2 * C_HD)); v_p.append(vp)
            k_s.append(kn.reshape(bsz_s, t_s, C_HEADS, 2 * C_HD)); v_s.append(vn)
        xp = xp + gp * op
        xs = xs + gs * os_
    y_prompt = rmsnorm(xp, final_g)
    y_sample = rmsnorm(xs, final_g)
    return (y_prompt, y_sample, jnp.stack(wkv_p), jnp.stack(shift_p), jnp.stack(k_p), jnp.stack(v_p),
            jnp.stack(wkv_s), jnp.stack(shift_s), jnp.stack(av_s), jnp.stack(k_s), jnp.stack(v_s))
```

```python
import functools
import math

import jax
import jax.numpy as jnp
from jax import lax
from jax.experimental import pallas as pl
from jax.experimental.pallas import tpu as pltpu

F32 = jnp.float32
BF16 = jnp.bfloat16

D_MODEL = 1024
A_WIDTH = 512
A_GROUPS = 4
A_GROUP_DIM = 128
CHUNK = 128
B_WIDTH = 512
B_HS = 64
B_HEADS = 8
B_PAIRS = B_HEADS // 2
LORA = 64
B_SHIFT = 3 * B_WIDTH + 2 * LORA
EVEN_COLS = 3 * A_WIDTH + B_SHIFT + B_WIDTH
PB_OFF = 3 * A_WIDTH
BZ_OFF = PB_OFF + B_SHIFT
C_HD = 64
C_VD = 128
C_HEADS = 8
C_WIDTH = 1024
ROT_DIM = 16
ROPE_THETA = 500000.0
ODD_COLS = 4096
PAGE_SIZE = 128
EPS = 1e-6
LN_EPS = 1e-5
GN_EPS = 64e-5
SCAN_L = 64
VMEM_LIMIT_BYTES = 56 * 1024 * 1024


def _cparams(*sem):
    return pltpu.CompilerParams(dimension_semantics=sem, vmem_limit_bytes=VMEM_LIMIT_BYTES)


def _bdot(a, b):
    return jnp.dot(a.astype(BF16), b.astype(BF16), preferred_element_type=F32)


def _bdot_nt(a, b):
    return lax.dot_general(a.astype(BF16), b.astype(BF16), (((1,), (1,)), ((), ())),
                           preferred_element_type=F32)


def _split2(x):
    hi = x.astype(BF16)
    lo = (x - hi.astype(F32)).astype(BF16)
    return hi, lo


def _split3(x):
    h1 = x.astype(BF16)
    r1 = x - h1.astype(F32)
    h2 = r1.astype(BF16)
    h3 = (r1 - h2.astype(F32)).astype(BF16)
    return h1, h2, h3


def _sigmoid(x):
    return 1.0 / (1.0 + jnp.exp(-x))


def _silu(x):
    return x * _sigmoid(x)


def _gelu(x):
    return 0.5 * x * (1.0 + jnp.tanh(math.sqrt(2.0 / math.pi) * (x + 0.044715 * (x * x * x))))


def _softplus(x):
    return jnp.maximum(x, 0.0) + jnp.log(1.0 + jnp.exp(-jnp.abs(x)))


def _modnorm(x, g, scale, shift):
    y = x * lax.rsqrt(jnp.mean(x * x, axis=-1, keepdims=True) + EPS)
    return (y * g) * (1.0 + scale) + shift


def _ada_kernel(c_ref, w_ref, b_ref, o_ref):
    o_ref[0] = _bdot(_silu(c_ref[...]), w_ref[0]) + b_ref[0]


def _ada_mod(c_all, ada_w, ada_b):
    depth, d, n3 = ada_w.shape
    m = c_all.shape[0]
    tn = 1024
    return pl.pallas_call(
        _ada_kernel,
        grid=(depth, n3 // tn),
        in_specs=[pl.BlockSpec((m, d), lambda l, j: (0, 0)),
                  pl.BlockSpec((1, d, tn), lambda l, j: (l, 0, j)),
                  pl.BlockSpec((1, 1, tn), lambda l, j: (l, 0, j))],
        out_specs=pl.BlockSpec((1, m, tn), lambda l, j: (l, 0, j)),
        out_shape=jax.ShapeDtypeStruct((depth, m, n3), F32),
        compiler_params=_cparams("parallel", "parallel"),
        name="ada_mod",
    )(c_all, ada_w, ada_b.reshape(depth, 1, n3))


def _a_branch_pre(p_u, p_v, p_z, vng, vnb):
    u = _gelu(p_u)
    vg = _gelu(p_v)
    vns = []
    for g in range(A_GROUPS):
        seg = vg[:, g * A_GROUP_DIM:(g + 1) * A_GROUP_DIM]
        mu = jnp.mean(seg, axis=-1, keepdims=True)
        xc = seg - mu
        var = jnp.mean(xc * xc, axis=-1, keepdims=True)
        vns.append(xc * lax.rsqrt(var + LN_EPS))
    vn = jnp.concatenate(vns, axis=-1) * vng + vnb
    return u, vn, _silu(p_z)


def _head_sum(x, gmat):
    hi, lo = _split2(x)
    return (jnp.dot(hi, gmat, preferred_element_type=F32)
            + jnp.dot(lo, gmat, preferred_element_type=F32))


def _b_branch_pre(pb, prev, p_bz, mu, w0, w2a, a0, kkw, kaw, rk, gmat):
    xs = pb + mu * (prev - pb)
    r = xs[:, 0:B_WIDTH]
    k = xs[:, B_WIDTH:2 * B_WIDTH]
    v = xs[:, 2 * B_WIDTH:3 * B_WIDTH]
    wa = xs[:, 3 * B_WIDTH:3 * B_WIDTH + 2 * LORA]
    lane = lax.broadcasted_iota(jnp.int32, wa.shape, 1)
    z = jnp.where(lane < LORA, jnp.tanh(wa), wa)
    lora = _bdot(z, w2a)
    w = -_softplus(-(w0 + lora[:, 0:B_WIDTH])) - 0.5
    logd = -jnp.exp(w)
    a = _sigmoid(a0 + lora[:, B_WIDTH:2 * B_WIDTH])
    kk = k * kkw
    kkn = kk / jnp.maximum(jnp.sqrt(_head_sum(kk * kk, gmat)), 1e-12)
    kmod = k * (1.0 + (a - 1.0) * kaw)
    bonus = _head_sum(r * kmod * rk, gmat) * v
    return r, logd, kmod, v, kkn, kkn * a, _silu(p_bz), bonus


def _even_in_kernel(x_ref, scale_ref, shift_ref, g_ref, w_ref, vng_ref, vnb_ref, ws_ref, bst_ref,
                    mu_ref, w0_ref, w2a_ref, a0_ref, kkw_ref, kaw_ref, rk_ref, gmat_ref,
                    outa_ref, r_ref, logd_ref, kmod_ref, v_ref, kkn_ref, beta_ref, gz_ref, bonus_ref,
                    shift_out_ref, carry_ref):
    i = pl.program_id(1)
    tm = x_ref.shape[1]

    @pl.when(i == 0)
    def _():
        carry_ref[...] = jnp.zeros_like(carry_ref)

    h = _modnorm(x_ref[0], g_ref[...], scale_ref[0], shift_ref[0]).astype(BF16)

    def proj(lo, width):
        return jnp.dot(h, w_ref[:, lo:lo + width], preferred_element_type=F32)

    u, vn, gate_a = _a_branch_pre(proj(0, A_WIDTH), proj(A_WIDTH, A_WIDTH), proj(2 * A_WIDTH, A_WIDTH),
                                  vng_ref[...], vnb_ref[...])
    ri = lax.broadcasted_iota(jnp.int32, (CHUNK, CHUNK), 0)
    ci = lax.broadcasted_iota(jnp.int32, (CHUNK, CHUNK), 1)
    vnb16 = vn.astype(BF16)
    cols = []
    for g in range(A_GROUPS):
        wg = jnp.where(ci <= ri, ws_ref[g], 0.0).astype(BF16)
        bias = bst_ref[:, g:g + 1]
        rows = []
        for c in range(tm // CHUNK):
            seg = vnb16[c * CHUNK:(c + 1) * CHUNK, g * A_GROUP_DIM:(g + 1) * A_GROUP_DIM]
            rows.append(jnp.dot(wg, seg, preferred_element_type=F32) + bias)
        cols.append(jnp.concatenate(rows, axis=0))
    mix = jnp.concatenate(cols, axis=-1)
    outa_ref[0] = (u * mix * gate_a).astype(outa_ref.dtype)

    pb = proj(PB_OFF, B_SHIFT)
    row = lax.broadcasted_iota(jnp.int32, pb.shape, 0)
    prev = jnp.where(row == 0, carry_ref[...], pltpu.roll(pb, 1, 0))
    last = pb[tm - 1:tm, :]
    carry_ref[...] = last
    shift_out_ref[0] = last
    outs = _b_branch_pre(pb, prev, proj(BZ_OFF, B_WIDTH), mu_ref[...], w0_ref[...], w2a_ref[...],
                         a0_ref[...], kkw_ref[...], kaw_ref[...], rk_ref[...], gmat_ref[...])
    for ref, val in zip((r_ref, logd_ref, kmod_ref, v_ref, kkn_ref, beta_ref, gz_ref, bonus_ref), outs):
        ref[0] = val


def _row(v):
    return v.reshape(1, -1)


def _even_weights(e, even_w_in, a_vn_g, a_vn_b, a_ws, a_bs, b_mu, b_w0, b_w2, b_a0, b_a2, b_kk, b_ka, b_rk):
    zeros = jnp.zeros((LORA, B_WIDTH), F32)
    w2a = jnp.concatenate([jnp.concatenate([b_w2[e], zeros], axis=1),
                           jnp.concatenate([zeros, b_a2[e]], axis=1)], axis=0).astype(BF16)
    hid = jnp.arange(B_WIDTH) // B_HS
    gmat = (hid[:, None] == hid[None, :]).astype(BF16)
    return dict(w_in=even_w_in[e].astype(BF16), vng=_row(a_vn_g[e]), vnb=_row(a_vn_b[e]), ws=a_ws[e],
                bst=a_bs[e].T, w00=_row(jnp.repeat(a_ws[e][:, 0, 0], A_GROUP_DIM)),
                bs0=_row(jnp.repeat(a_bs[e][:, 0], A_GROUP_DIM)), mu=_row(b_mu[e]), w0=_row(b_w0[e]), w2a=w2a, a0=_row(b_a0[e]),
                kkw=_row(b_kk[e]), kaw=_row(b_ka[e]), rk=_row(b_rk[e]), gmat=gmat)


def _full(shape):
    nd = len(shape)
    return pl.BlockSpec(shape, lambda *_: (0,) * nd)


def _even_in_prompt(x, scale, shift, norm_g, ew, tm=256):
    bsz, t, d = x.shape
    tok = lambda w: pl.BlockSpec((1, tm, w), lambda b, i: (b, i, 0))
    vec = pl.BlockSpec((1, 1, d), lambda b, i: (b, 0, 0))
    f32out = jax.ShapeDtypeStruct((bsz, t, B_WIDTH), F32)
    return pl.pallas_call(
        _even_in_kernel,
        grid=(bsz, t // tm),
        in_specs=[tok(d), vec, vec, _full((1, d)), _full((d, EVEN_COLS)), _full((1, A_WIDTH)),
                  _full((1, A_WIDTH)), _full((A_GROUPS, CHUNK, CHUNK)), _full((CHUNK, A_GROUPS)),
                  _full((1, B_SHIFT)), _full((1, B_WIDTH)), _full((2 * LORA, 2 * B_WIDTH)),
                  _full((1, B_WIDTH)), _full((1, B_WIDTH)), _full((1, B_WIDTH)), _full((1, B_WIDTH)),
                  _full((B_WIDTH, B_WIDTH))],
        out_specs=[tok(A_WIDTH)] + [tok(B_WIDTH)] * 8 + [pl.BlockSpec((1, 1, B_SHIFT), lambda b, i: (b, 0, 0))],
        out_shape=[jax.ShapeDtypeStruct((bsz, t, A_WIDTH), BF16)] + [f32out] * 8
                  + [jax.ShapeDtypeStruct((bsz, 1, B_SHIFT), F32)],
        scratch_shapes=[pltpu.VMEM((1, B_SHIFT), F32)],
        compiler_params=_cparams("parallel", "arbitrary"),
        name="even_in_prompt",
    )(x, scale, shift, _row(norm_g), ew["w_in"], ew["vng"], ew["vnb"], ew["ws"], ew["bst"], ew["mu"],
      ew["w0"], ew["w2a"], ew["a0"], ew["kkw"], ew["kaw"], ew["rk"], ew["gmat"])


def _scan_kernel(r_ref, logd_ref, kmod_ref, v_ref, kkn_ref, beta_ref, gz_ref, bonus_ref, gng_ref, gnb_ref,
                 outb_ref, sfin_ref, s_ref):
    ib = pl.program_id(1)
    n_chunks = r_ref.shape[1] // SCAN_L
    L, L2 = SCAN_L, 2 * SCAN_L

    @pl.when(ib == 0)
    def _():
        s_ref[...] = jnp.zeros_like(s_ref)

    ri = lax.broadcasted_iota(jnp.int32, (L2, L2), 0)
    ci = lax.broadcasted_iota(jnp.int32, (L2, L2), 1)
    same_head = (ri // L) == (ci // L)
    strict = same_head & (ci < ri)
    incl = same_head & (ci <= ri)
    eye = (ri == ci).astype(F32)
    tri = (lax.broadcasted_iota(jnp.int32, (L, L), 1) <= lax.broadcasted_iota(jnp.int32, (L, L), 0))
    tri3 = jnp.concatenate([tri.astype(BF16)] * 3, axis=1)

    def stack(x):
        lane = lax.broadcasted_iota(jnp.int32, x.shape, 1)
        return jnp.concatenate([jnp.where(lane < B_HS, x, 0.0), jnp.where(lane >= B_HS, x, 0.0)], axis=0)

    def dup(x):
        return jnp.concatenate([x, x], axis=0)

    def chunk(c, carry):
        sl = pl.ds(pl.multiple_of(c * L, L), L)
        logd = logd_ref[0, sl, :]
        cs = jnp.dot(tri3, jnp.concatenate(_split3(logd), axis=0), preferred_element_type=F32)
        cs_last = cs[L - 1:L, :]
        p = jnp.exp(cs)
        pinv = jnp.exp(-cs)
        at = -kkn_ref[0, sl, :] * jnp.exp(cs - logd)
        rt = r_ref[0, sl, :] * p
        beta = beta_ref[0, sl, :]
        kmod = kmod_ref[0, sl, :]
        bt = beta * pinv
        kt = kmod * pinv
        to_end = jnp.exp(cs_last - cs)
        bl = beta * to_end
        kl = kmod * to_end
        p_last = p[L - 1:L, :]
        vv = v_ref[0, sl, :]
        for j in range(B_PAIRS):
            ln = slice(2 * B_HS * j, 2 * B_HS * (j + 1))
            s = s_ref[j]
            s_hi, s_lo = _split2(s)
            lhs = jnp.concatenate([stack(at[:, ln]), stack(rt[:, ln])], axis=0).astype(BF16)
            rhs = jnp.concatenate([dup(bt[:, ln]), dup(kt[:, ln])], axis=0).astype(BF16)
            m1 = lax.dot_general(lhs, rhs, (((1,), (1,)), ((), ())), preferred_element_type=F32)
            nbd = jnp.where(strict, m1[0:L2, 0:L2], 0.0)
            akbd = jnp.where(strict, m1[0:L2, L2:2 * L2], 0.0)
            rbbd = jnp.where(incl, m1[L2:2 * L2, 0:L2], 0.0)
            rkbd = jnp.where(incl, m1[L2:2 * L2, L2:2 * L2], 0.0)
            g0 = (lax.dot_general(lhs, s_hi, (((1,), (1,)), ((), ())), preferred_element_type=F32)
                  + lax.dot_general(lhs, s_lo, (((1,), (1,)), ((), ())), preferred_element_type=F32))
            v_st = stack(vv[:, ln])
            v_dup = dup(vv[:, ln]).astype(BF16)
            st_mask = stack(jnp.ones((L, 2 * B_HS), F32))
            rhs_u = g0[0:L2] + jnp.dot(akbd.astype(BF16), v_dup, preferred_element_type=F32) * st_mask
            tmat = eye + nbd
            pw = _hp_dot(nbd, nbd)
            steps = int(math.log2(L)) - 1
            for k in range(steps):
                if k + 1 < steps:
                    both = _hp_dot(pw, jnp.concatenate([pw, tmat], axis=1))
                    tmat = tmat + both[:, L2:]
                    pw = both[:, 0:L2]
                else:
                    tmat = tmat + _hp_dot(pw, tmat)
            u_st = _hp_dot(tmat, rhs_u)
            y_st = (g0[L2:] + _bdot(rbbd, u_st) + jnp.dot(rkbd.astype(BF16), v_dup,
                                                            preferred_element_type=F32) * st_mask)
            y = y_st[0:L] + y_st[L:]
            uv = jnp.concatenate([u_st, v_st], axis=0)
            bk = jnp.concatenate([stack(bl[:, ln]), stack(kl[:, ln])], axis=0)
            s_ref[j] = s * p_last[:, ln] + _hp_dot(uv.T, bk)
            lane = lax.broadcasted_iota(jnp.int32, y.shape, 1)
            h0 = lane < B_HS
            def hmean(z):
                m0 = jnp.sum(jnp.where(h0, z, 0.0), axis=-1, keepdims=True)
                m1_ = jnp.sum(jnp.where(h0, 0.0, z), axis=-1, keepdims=True)
                return jnp.where(h0, m0, m1_) * (1.0 / B_HS)
            yc = y - hmean(y)
            yn = yc * lax.rsqrt(hmean(yc * yc) + GN_EPS) * gng_ref[:, ln] + gnb_ref[:, ln]
            outb_ref[0, sl, ln] = ((yn + bonus_ref[0, sl, ln]) * gz_ref[0, sl, ln]).astype(outb_ref.dtype)
        return carry

    lax.fori_loop(0, n_chunks, chunk, 0)

    @pl.when(ib == pl.num_programs(1) - 1)
    def _():
        for j in range(B_PAIRS):
            s = s_ref[j]
            sfin_ref[0, 2 * j] = s[0:B_HS, 0:B_HS]
            sfin_ref[0, 2 * j + 1] = s[B_HS:, B_HS:]


def _hp_dot(a, b):
    a_hi, a_lo = _split2(a)
    b_hi, b_lo = _split2(b)
    return (jnp.dot(a_hi, b_hi, preferred_element_type=F32) + jnp.dot(a_hi, b_lo, preferred_element_type=F32)
            + jnp.dot(a_lo, b_hi, preferred_element_type=F32))


def _scan_prompt(r, logd, kmod, v, kkn, beta, gz, bonus, gn_g, gn_b, tb=256):
    bsz, t, w = r.shape
    tok = pl.BlockSpec((1, tb, w), lambda b, i: (b, i, 0))
    return pl.pallas_call(
        _scan_kernel,
        grid=(bsz, t // tb),
        in_specs=[tok] * 8 + [_full((1, w)), _full((1, w))],
        out_specs=[tok, pl.BlockSpec((1, B_HEADS, B_HS, B_HS), lambda b, i: (b, 0, 0, 0))],
        out_shape=[jax.ShapeDtypeStruct((bsz, t, w), BF16),
                   jax.ShapeDtypeStruct((bsz, B_HEADS, B_HS, B_HS), F32)],
        scratch_shapes=[pltpu.VMEM((B_PAIRS, 2 * B_HS, 2 * B_HS), F32)],
        compiler_params=_cparams("parallel", "arbitrary"),
        name="rwkv7_scan_prompt",
    )(r, logd, kmod, v, kkn, beta, gz, bonus, _row(gn_g), _row(gn_b))


def _out_proj_kernel(*refs, n_in, final):
    ins = refs[:n_in]
    w_ref, x_ref, gate_ref = refs[n_in:n_in + 3]
    rest = refs[n_in + 3:]
    acc = None
    off = 0
    for a_ref in ins:
        wd = a_ref.shape[-1]
        part = jnp.dot(a_ref[0], w_ref[off:off + wd, :], preferred_element_type=F32)
        acc = part if acc is None else acc + part
        off += wd
    xn = x_ref[0] + gate_ref[0] * acc
    if final:
        fg_ref, y_ref = rest
        y_ref[0] = xn * lax.rsqrt(jnp.mean(xn * xn, axis=-1, keepdims=True) + EPS) * fg_ref[...]
    else:
        (xo_ref,) = rest
        xo_ref[0] = xn


def _out_proj(acts, w_out, x, gate, final_g=None, tm=512):
    bsz, t, d = x.shape
    tm = min(tm, t)
    final = final_g is not None
    tok = lambda w: pl.BlockSpec((1, tm, w), lambda b, i: (b, i, 0))
    gate_spec = tok(d) if gate.shape[1] == t and t > 1 else pl.BlockSpec((1, 1, d), lambda b, i: (b, 0, 0))
    in_specs = [tok(a.shape[-1]) for a in acts] + [_full(w_out.shape), tok(d), gate_spec]
    args = list(acts) + [w_out, x, gate]
    if final:
        in_specs.append(_full((1, d)))
        args.append(_row(final_g))
    return pl.pallas_call(
        functools.partial(_out_proj_kernel, n_in=len(acts), final=final),
        grid=(bsz, t // tm),
        in_specs=in_specs, out_specs=tok(d), out_shape=jax.ShapeDtypeStruct((bsz, t, d), F32),
        compiler_params=_cparams("parallel", "parallel"),
        name="out_proj",
    )(*args)


def _rope_tables(pos):
    half = ROT_DIM // 2
    inv = ROPE_THETA ** (-jnp.arange(half, dtype=F32) / half)
    ang = pos.astype(F32)[:, None] * inv[None]
    cos, sin = jnp.cos(ang), jnp.sin(ang)
    n = pos.shape[0]
    pad_c = jnp.ones((n, C_HD - ROT_DIM), F32)
    pad_s = jnp.zeros((n, C_HD - ROT_DIM), F32)
    cos_h = jnp.concatenate([cos, cos, pad_c], axis=1)
    sin_h = jnp.concatenate([-sin, sin, pad_s], axis=1)
    return jnp.concatenate([cos_h, cos_h], axis=1), jnp.concatenate([sin_h, sin_h], axis=1)


def _rope(x, cos_t, sin_t):
    half = ROT_DIM // 2
    lane = lax.broadcasted_iota(jnp.int32, x.shape, 1) % C_HD
    partner = jnp.where(lane < half, pltpu.roll(x, 2 * C_HD - half, 1), pltpu.roll(x, half, 1))
    return x * cos_t + partner * sin_t


def _odd_in_kernel(x_ref, scale_ref, shift_ref, g_ref, w_ref, cos_ref, sin_ref,
                   q_ref, k_ref, v_ref, kb_ref, vb_ref, gz_ref):
    h = _modnorm(x_ref[0], g_ref[...], scale_ref[0], shift_ref[0]).astype(BF16)
    cos_t, sin_t = cos_ref[...], sin_ref[...]
    qk = C_HEADS * 2 * C_HD
    for hd in range(C_HEADS):
        ln = slice(hd * 2 * C_HD, (hd + 1) * 2 * C_HD)
        q = jnp.dot(h, w_ref[:, hd * 2 * C_HD:(hd + 1) * 2 * C_HD], preferred_element_type=F32)
        q_ref[0, :, ln] = (_rope(q, cos_t, sin_t) * (C_HD ** -0.5)).astype(q_ref.dtype)
        k = jnp.dot(h, w_ref[:, qk + hd * 2 * C_HD:qk + (hd + 1) * 2 * C_HD], preferred_element_type=F32)
        k = _rope(k, cos_t, sin_t)
        k_ref[0, :, ln] = k
        kb_ref[0, :, ln] = k.astype(BF16)
    v = jnp.dot(h, w_ref[:, 2 * qk:2 * qk + C_WIDTH], preferred_element_type=F32)
    v_ref[0] = v
    vb_ref[0] = v.astype(BF16)
    gz_ref[0] = _silu(jnp.dot(h, w_ref[:, 2 * qk + C_WIDTH:], preferred_element_type=F32))


def _odd_in(x, scale, shift, norm_g, w_in, cos_t, sin_t, q_dtype, tm=256):
    bsz, t, d = x.shape
    tm = min(tm, t)
    tok = lambda w: pl.BlockSpec((1, tm, w), lambda b, i: (b, i, 0))
    vec = tok(d) if scale.shape[1] == t and t > 1 else pl.BlockSpec((1, 1, d), lambda b, i: (b, 0, 0))
    if cos_t.shape[0] == 1:
        tab = _full((1, 2 * C_HD))
    else:
        tab = pl.BlockSpec((tm, 2 * C_HD), lambda b, i: (i, 0))
    out = lambda dt: jax.ShapeDtypeStruct((bsz, t, C_WIDTH), dt)
    return pl.pallas_call(
        _odd_in_kernel,
        grid=(bsz, t // tm),
        in_specs=[tok(d), vec, vec, _full((1, d)), _full((d, ODD_COLS)), tab, tab],
        out_specs=[tok(C_WIDTH)] * 6,
        out_shape=[out(q_dtype), out(F32), out(F32), out(BF16), out(BF16), out(F32)],
        compiler_params=_cparams("parallel", "parallel"),
        name="odd_in",
    )(x, scale, shift, _row(norm_g), w_in, cos_t, sin_t)


def _diff_lambda(lq1, lk1, lq2, lk2, lam_init):
    return (jnp.exp(jnp.sum(lq1 * lk1, axis=-1, keepdims=True))
            - jnp.exp(jnp.sum(lq2 * lk2, axis=-1, keepdims=True)) + lam_init)


def _attn_finish(o1, o2, lam, subln_g, gz, lam_init):
    o = o1 - lam * o2
    o = o * lax.rsqrt(jnp.mean(o * o, axis=-1, keepdims=True) + EPS) * subln_g
    return o * (1.0 - lam_init) * gz


def _attn_kernel(qi_ref, ki_ref, q_ref, k_ref, v_ref, gz_ref, lq1_ref, lk1_ref, lq2_ref, lk2_ref, g_ref,
                 o_ref, qm_ref, m_ref, l_ref, acc_ref, *, lam_init):
    pid = pl.program_id(2)
    qi = qi_ref[pid]
    ki = ki_ref[pid]
    tq, tk = q_ref.shape[1], k_ref.shape[1]

    @pl.when(ki == 0)
    def _():
        q = q_ref[0]
        lane = lax.broadcasted_iota(jnp.int32, q.shape, 1)
        qm_ref[0] = jnp.where(lane < C_HD, q, jnp.zeros_like(q))
        qm_ref[1] = jnp.where(lane >= C_HD, q, jnp.zeros_like(q))
        m_ref[...] = jnp.full_like(m_ref, -jnp.inf)
        l_ref[...] = jnp.zeros_like(l_ref)
        acc_ref[...] = jnp.zeros_like(acc_ref)

    def step(diagonal):
        k = k_ref[0]
        v = v_ref[0]
        for idx in range(2):
            s = lax.dot_general(qm_ref[idx], k, (((1,), (1,)), ((), ())), preferred_element_type=F32)
            if diagonal:
                row = lax.broadcasted_iota(jnp.int32, s.shape, 0)
                col = lax.broadcasted_iota(jnp.int32, s.shape, 1)
                s = jnp.where(col <= row, s, -jnp.inf)
            m_old = m_ref[idx]
            m_new = jnp.maximum(m_old, jnp.max(s, axis=-1, keepdims=True))
            alpha = jnp.exp(m_old - m_new)
            p = jnp.exp(s - m_new)
            l_ref[idx] = alpha * l_ref[idx] + jnp.sum(p, axis=-1, keepdims=True)
            acc_ref[idx] = alpha * acc_ref[idx] + jnp.dot(p.astype(BF16), v, preferred_element_type=F32)
            m_ref[idx] = m_new

    @pl.when(ki < qi)
    def _():
        step(False)

    @pl.when(ki == qi)
    def _():
        step(True)
        lam = _diff_lambda(lq1_ref[...], lk1_ref[...], lq2_ref[...], lk2_ref[...], lam_init)
        out = _attn_finish(acc_ref[0] / l_ref[0], acc_ref[1] / l_ref[1], lam, g_ref[...], gz_ref[0], lam_init)
        o_ref[0] = out.astype(o_ref.dtype)


def _diff_attn_prompt(q, kb, vb, gz, lq1, lk1, lq2, lk2, subln_g, lam_init, tq=512):
    bsz, t, _ = q.shape
    tq = min(tq, t)
    nq = t // tq
    pairs = [(i, j) for i in range(nq) for j in range(i + 1)]
    qi_of = jnp.asarray([p[0] for p in pairs], jnp.int32)
    ki_of = jnp.asarray([p[1] for p in pairs], jnp.int32)
    hw = 2 * C_HD
    qspec = pl.BlockSpec((1, tq, hw), lambda b, h, p, qi, ki: (b, qi[p], h))
    kspec = pl.BlockSpec((1, tq, hw), lambda b, h, p, qi, ki: (b, ki[p], h))
    small = lambda w: pl.BlockSpec((1, w), lambda b, h, p, qi, ki: (0, 0))
    grid_spec = pltpu.PrefetchScalarGridSpec(
        num_scalar_prefetch=2,
        grid=(bsz, C_HEADS, len(pairs)),
        in_specs=[qspec, kspec, kspec, qspec, small(C_HD), small(C_HD), small(C_HD), small(C_HD), small(C_VD)],
        out_specs=qspec,
        scratch_shapes=[pltpu.VMEM((2, tq, hw), BF16), pltpu.VMEM((2, tq, 1), F32),
                        pltpu.VMEM((2, tq, 1), F32), pltpu.VMEM((2, tq, C_VD), F32)],
    )
    return pl.pallas_call(
        functools.partial(_attn_kernel, lam_init=lam_init),
        grid_spec=grid_spec,
        out_shape=jax.ShapeDtypeStruct((bsz, t, C_WIDTH), BF16),
        compiler_params=_cparams("parallel", "parallel", "arbitrary"),
        name="diff_attn_prompt",
    )(qi_of, ki_of, q, kb, vb, gz, _row(lq1), _row(lk1), _row(lq2), _row(lk2), _row(subln_g))


def _even_in_sample_kernel(x_ref, scale_ref, shift_ref, g_ref, w_ref, vng_ref, vnb_ref, w00_ref, bs0_ref,
                           prev_ref, mu_ref, w0_ref, w2a_ref, a0_ref, kkw_ref, kaw_ref, rk_ref, gmat_ref,
                           outa_ref, vn_ref, pb_ref, *t_refs):
    h = _modnorm(x_ref[...], g_ref[...], scale_ref[...], shift_ref[...]).astype(BF16)

    def proj(lo, width):
        return jnp.dot(h, w_ref[:, lo:lo + width], preferred_element_type=F32)

    u, vn, gate_a = _a_branch_pre(proj(0, A_WIDTH), proj(A_WIDTH, A_WIDTH), proj(2 * A_WIDTH, A_WIDTH),
                                  vng_ref[...], vnb_ref[...])
    vn_ref[...] = vn
    mix = vn.astype(BF16).astype(F32) * w00_ref[...].astype(BF16).astype(F32) + bs0_ref[...]
    outa_ref[...] = (u * mix * gate_a).astype(outa_ref.dtype)

    pb = proj(PB_OFF, B_SHIFT)
    pb_ref[...] = pb
    outs = _b_branch_pre(pb, prev_ref[...], proj(BZ_OFF, B_WIDTH), mu_ref[...], w0_ref[...], w2a_ref[...],
                         a0_ref[...], kkw_ref[...], kaw_ref[...], rk_ref[...], gmat_ref[...])
    for ref, val in zip(t_refs, outs):
        ref[...] = val.T


def _even_in_sample(x, scale, shift, norm_g, ew, prev):
    n, d = x.shape
    t_out = jax.ShapeDtypeStruct((B_WIDTH, n), F32)
    args = (x, scale, shift, _row(norm_g), ew["w_in"], ew["vng"], ew["vnb"], ew["w00"], ew["bs0"], prev,
            ew["mu"], ew["w0"], ew["w2a"], ew["a0"], ew["kkw"], ew["kaw"], ew["rk"], ew["gmat"])
    out_shape = [jax.ShapeDtypeStruct((n, A_WIDTH), BF16), jax.ShapeDtypeStruct((n, A_WIDTH), F32),
                 jax.ShapeDtypeStruct((n, B_SHIFT), F32)] + [t_out] * 8
    return pl.pallas_call(
        _even_in_sample_kernel,
        grid=(1,),
        in_specs=[_full(a.shape) for a in args],
        out_specs=[_full(s.shape) for s in out_shape],
        out_shape=out_shape,
        compiler_params=_cparams("arbitrary"),
        name="even_in_sample",
    )(*args)


def _wkv_sample_kernel(s_ref, r_ref, logd_ref, kmod_ref, v_ref, kkn_ref, beta_ref, gz_ref, bonus_ref,
                       gng_ref, gnb_ref, snew_ref, outb_ref, st_ref, y_ref):
    st_ref[...] = s_ref[...].T
    neg_kk = -kkn_ref[...]
    decay = jnp.exp(logd_ref[...])
    beta = beta_ref[...]
    kmod = kmod_ref[...]
    r = r_ref[...]

    def body(vi, carry):
        rows = pl.ds(pl.multiple_of(vi * B_HS, B_HS), B_HS)
        slab = st_ref[rows, :]
        sa = jnp.sum(slab * neg_kk, axis=0, keepdims=True)
        slab = slab * decay + sa * beta + v_ref[pl.ds(vi, 1), :] * kmod
        st_ref[rows, :] = slab
        y_ref[pl.ds(vi, 1), :] = jnp.sum(slab * r, axis=0, keepdims=True)
        return carry

    lax.fori_loop(0, B_HS, body, 0)
    snew_ref[...] = st_ref[...].T
    y = y_ref[...]
    yc = y - jnp.mean(y, axis=0, keepdims=True)
    yn = yc * lax.rsqrt(jnp.mean(yc * yc, axis=0, keepdims=True) + GN_EPS) * gng_ref[...] + gnb_ref[...]
    outb_ref[...] = (yn + bonus_ref[...]) * gz_ref[...]


def _wkv_sample(state, t_ins, gn_g, gn_b):
    n = state.shape[0]
    hs2 = B_HS * B_HS
    sblk = pl.BlockSpec((n, hs2), lambda h: (0, h))
    tblk = pl.BlockSpec((B_HS, n), lambda h: (h, 0))
    cblk = pl.BlockSpec((B_HS, 1), lambda h: (h, 0))
    return pl.pallas_call(
        _wkv_sample_kernel,
        grid=(B_HEADS,),
        in_specs=[sblk] + [tblk] * 8 + [cblk, cblk],
        out_specs=[sblk, tblk],
        out_shape=[jax.ShapeDtypeStruct(state.shape, F32), jax.ShapeDtypeStruct((B_WIDTH, n), F32)],
        scratch_shapes=[pltpu.VMEM((hs2, n), F32), pltpu.VMEM((B_HS, n), F32)],
        compiler_params=_cparams("parallel"),
        name="wkv_sample",
    )(state, *t_ins, gn_g.reshape(-1, 1), gn_b.reshape(-1, 1))


def _out_proj_sample_kernel(outa_ref, outbt_ref, w_ref, x_ref, gate_ref, xo_ref):
    acc = (jnp.dot(outa_ref[...], w_ref[0:A_WIDTH, :], preferred_element_type=F32)
           + jnp.dot(outbt_ref[...].T.astype(BF16), w_ref[A_WIDTH:, :], preferred_element_type=F32))
    xo_ref[...] = x_ref[...] + gate_ref[...] * acc


def _out_proj_sample(outa, outbt, w_out, x, gate):
    args = (outa, outbt, w_out, x, gate)
    return pl.pallas_call(
        _out_proj_sample_kernel,
        grid=(1,),
        in_specs=[_full(a.shape) for a in args],
        out_specs=_full(x.shape),
        out_shape=jax.ShapeDtypeStruct(x.shape, F32),
        compiler_params=_cparams("arbitrary"),
        name="out_proj_sample",
    )(*args)


def _paged_attn_kernel(pt_ref, q_ref, kn_ref, vn_ref, gz_ref, lq1_ref, lk1_ref, lq2_ref, lk2_ref, g_ref,
                       *rest, pages_per_step, lam_init):
    del pt_ref
    k_refs = rest[:pages_per_step]
    v_refs = rest[pages_per_step:2 * pages_per_step]
    o_ref, m_ref, l_ref, acc_ref = rest[2 * pages_per_step:]
    g = pl.program_id(1)

    @pl.when(g == 0)
    def _():
        m_ref[...] = jnp.full_like(m_ref, -jnp.inf)
        l_ref[...] = jnp.zeros_like(l_ref)
        acc_ref[...] = jnp.zeros_like(acc_ref)

    q = q_ref[0]
    lane = lax.broadcasted_iota(jnp.int32, q.shape, 1)
    halves = (lane < C_HD, lane >= C_HD)

    def absorb(kp, vp):
        prod = kp * q[None]
        for idx in range(2):
            s = jnp.sum(jnp.where(halves[idx][None], prod, 0.0), axis=-1, keepdims=True)
            m_old = m_ref[idx]
            m_new = jnp.maximum(m_old, jnp.max(s, axis=0))
            alpha = jnp.exp(m_old - m_new)
            p = jnp.exp(s - m_new[None])
            l_ref[idx] = alpha * l_ref[idx] + jnp.sum(p, axis=0)
            acc_ref[idx] = alpha * acc_ref[idx] + jnp.sum(p * vp, axis=0)
            m_ref[idx] = m_new

    for j in range(pages_per_step):
        absorb(k_refs[j][0, 0], v_refs[j][0, 0])

    @pl.when(g == pl.num_programs(1) - 1)
    def _():
        absorb(kn_ref[...], vn_ref[...])
        lam = _diff_lambda(lq1_ref[...], lk1_ref[...], lq2_ref[...], lk2_ref[...], lam_init)
        out = _attn_finish(acc_ref[0] / l_ref[0], acc_ref[1] / l_ref[1], lam, g_ref[...], gz_ref[0], lam_init)
        o_ref[0] = out.astype(o_ref.dtype)


def _paged_attn_sample(layer, page_table, q, k_new, v_new, gz, cache_k, cache_v, lq1, lk1, lq2, lk2, subln_g,
                       lam_init, pages_per_step=8):
    n, n_pages = page_table.shape
    steps = n_pages // pages_per_step
    tokblk = pl.BlockSpec((1, C_HEADS, 2 * C_HD), lambda b, g, pt: (b, 0, 0))
    small = lambda w: pl.BlockSpec((1, w), lambda b, g, pt: (0, 0))

    def page(j):
        return pl.BlockSpec((1, 1, PAGE_SIZE, C_HEADS, 2 * C_HD),
                            lambda b, g, pt: (layer, pt[b * n_pages + g * pages_per_step + j], 0, 0, 0))

    grid_spec = pltpu.PrefetchScalarGridSpec(
        num_scalar_prefetch=1,
        grid=(n, steps),
        in_specs=[tokblk] * 4 + [small(C_HD)] * 4 + [small(C_VD)]
                 + [page(j) for j in range(pages_per_step)] * 2,
        out_specs=tokblk,
        scratch_shapes=[pltpu.VMEM((2, C_HEADS, 1), F32), pltpu.VMEM((2, C_HEADS, 1), F32),
                        pltpu.VMEM((2, C_HEADS, C_VD), F32)],
    )
    return pl.pallas_call(
        functools.partial(_paged_attn_kernel, pages_per_step=pages_per_step, lam_init=lam_init),
        grid_spec=grid_spec,
        out_shape=jax.ShapeDtypeStruct((n, C_HEADS, C_VD), BF16),
        compiler_params=_cparams("parallel", "arbitrary"),
        name="paged_attn_sample",
    )(page_table.reshape(-1), q, k_new, v_new, gz, _row(lq1), _row(lk1), _row(lq2), _row(lk2), _row(subln_g),
      *([cache_k] * pages_per_step), *([cache_v] * pages_per_step))


def kernel(x_prompt, x_sample, c_prompt, c_sample, state_wkv, state_shift, cache_k, cache_v, page_table, ada_w, ada_b, norm_g, final_g, even_w_in, even_w_out, a_vn_g, a_vn_b, a_ws, a_bs, b_mu, b_w0, b_w2, b_a0, b_a2, b_kk, b_ka, b_rk, b_gn_g, b_gn_b, odd_w_in, odd_w_out, c_lq1, c_lk1, c_lq2, c_lk2, c_subln_g):
    depth = ada_w.shape[0]
    assert depth % 2 == 0, "the final RMSNorm is fused into the last (odd) layer's output projection"
    bsz_p, t_p, d = x_prompt.shape
    bsz_s = x_sample.shape[0]
    past = page_table.shape[1] * PAGE_SIZE

    rows = bsz_p + bsz_s
    pad = -rows % 8
    c_all = jnp.concatenate([c_prompt, c_sample, jnp.zeros((pad, d), F32)], axis=0)
    mod = _ada_mod(c_all, ada_w, ada_b)
    shift_p, scale_p, gate_p = [m[:, :bsz_p, None, :] for m in jnp.split(mod, 3, axis=-1)]
    shift_s, scale_s, gate_s = [m[:, bsz_p:rows, :] for m in jnp.split(mod, 3, axis=-1)]

    cos_p, sin_p = _rope_tables(jnp.arange(t_p))
    cos_s, sin_s = _rope_tables(past + jnp.arange(1))

    xp = x_prompt
    xs = x_sample.reshape(bsz_s, d)
    wkv_p, shp_p, k_p, v_p = [], [], [], []
    wkv_s, shp_s, av_s, k_s, v_s = [], [], [], [], []
    y_prompt = y_sample = None
    for l in range(depth):
        last = l == depth - 1
        fg = final_g if last else None
        if l % 2 == 0:
            e = l // 2
            ew = _even_weights(e, even_w_in, a_vn_g, a_vn_b, a_ws, a_bs, b_mu, b_w0, b_w2, b_a0, b_a2, b_kk,
                               b_ka, b_rk)
            w_out = even_w_out[e].astype(BF16)
            outs = _even_in_prompt(xp, scale_p[l], shift_p[l], norm_g[l], ew)
            outb, sfin = _scan_prompt(*outs[1:9], b_gn_g[e], b_gn_b[e])
            wkv_p.append(sfin)
            shp_p.append(outs[9][:, 0])
            xp = _out_proj([outs[0], outb], w_out, xp, gate_p[l], fg)
            souts = _even_in_sample(xs, scale_s[l], shift_s[l], norm_g[l], ew, state_shift[e])
            av_s.append(souts[1][:, None, :])
            shp_s.append(souts[2])
            snew, outbt = _wkv_sample(state_wkv[e].reshape(bsz_s, -1), souts[3:], b_gn_g[e], b_gn_b[e])
            wkv_s.append(snew.reshape(bsz_s, B_HEADS, B_HS, B_HS))
            xs = _out_proj_sample(souts[0], outbt, w_out, xs, gate_s[l])
        else:
            o = l // 2
            lam_init = 0.8 - 0.6 * math.exp(-0.3 * l)
            w_in = odd_w_in[o].astype(BF16)
            w_out = odd_w_out[o].astype(BF16)
            lam_args = (c_lq1[o], c_lk1[o], c_lq2[o], c_lk2[o], c_subln_g[o], lam_init)
            q, k, v, kb, vb, gz = _odd_in(xp, scale_p[l], shift_p[l], norm_g[l], w_in, cos_p, sin_p, BF16)
            k_p.append(k.reshape(bsz_p, t_p, C_HEADS, 2 * C_HD))
            v_p.append(v.reshape(bsz_p, t_p, C_HEADS, C_VD))
            att = _diff_attn_prompt(q, kb, vb, gz, *lam_args)
            xp = _out_proj([att], w_out, xp, gate_p[l], fg)
            q, k, v, _, _, gz = _odd_in(xs[None], scale_s[l][None], shift_s[l][None], norm_g[l], w_in,
                                        cos_s, sin_s, F32)
            hd = lambda z: z.reshape(bsz_s, C_HEADS, 2 * C_HD)
            k_s.append(k.reshape(bsz_s, 1, C_HEADS, 2 * C_HD))
            v_s.append(v.reshape(bsz_s, 1, C_HEADS, C_VD))
            att = _paged_attn_sample(o, page_table, hd(q), hd(k), hd(v), hd(gz), cache_k, cache_v, *lam_args)
            xs = _out_proj([att.reshape(1, bsz_s, C_WIDTH)], w_out, xs[None], gate_s[l][None], fg)[0]
            if last:
                y_prompt, y_sample = xp, xs
    return (y_prompt, y_sample.reshape(bsz_s, 1, d), jnp.stack(wkv_p), jnp.stack(shp_p), jnp.stack(k_p),
            jnp.stack(v_p), jnp.stack(wkv_s), jnp.stack(shp_s), jnp.stack(av_s), jnp.stack(k_s), jnp.stack(v_s))
```

```python
import functools
import math

import jax
import jax.numpy as jnp
from jax import lax
from jax.experimental import pallas as pl
from jax.experimental.pallas import tpu as pltpu

F32 = jnp.float32
BF16 = jnp.bfloat16

D_MODEL = 1024
A_WIDTH = 512
A_GROUPS = 4
A_GROUP_DIM = 128
CHUNK = 128
B_WIDTH = 512
B_HS = 64
B_HEADS = 8
B_PAIRS = B_HEADS // 2
LORA = 64
B_SHIFT = 3 * B_WIDTH + 2 * LORA
EVEN_COLS = 3 * A_WIDTH + B_SHIFT + B_WIDTH
PB_OFF = 3 * A_WIDTH
BZ_OFF = PB_OFF + B_SHIFT
C_HD = 64
C_VD = 128
C_HEADS = 8
C_WIDTH = 1024
ROT_DIM = 16
ROPE_THETA = 500000.0
ODD_COLS = 4096
PAGE_SIZE = 128
EPS = 1e-6
LN_EPS = 1e-5
GN_EPS = 64e-5
SCAN_L = 64
VMEM_LIMIT_BYTES = 56 * 1024 * 1024


def _cparams(*sem):
    return pltpu.CompilerParams(dimension_semantics=sem, vmem_limit_bytes=VMEM_LIMIT_BYTES)


def _bdot(a, b):
    return jnp.dot(a.astype(BF16), b.astype(BF16), preferred_element_type=F32)


def _bdot_nt(a, b):
    return lax.dot_general(a.astype(BF16), b.astype(BF16), (((1,), (1,)), ((), ())),
                           preferred_element_type=F32)


def _split2(x):
    hi = x.astype(BF16)
    lo = (x - hi.astype(F32)).astype(BF16)
    return hi, lo


def _split3(x):
    h1 = x.astype(BF16)
    r1 = x - h1.astype(F32)
    h2 = r1.astype(BF16)
    h3 = (r1 - h2.astype(F32)).astype(BF16)
    return h1, h2, h3


def _sigmoid(x):
    return 1.0 / (1.0 + jnp.exp(-x))


def _silu(x):
    return x * _sigmoid(x)


def _gelu(x):
    return 0.5 * x * (1.0 + jnp.tanh(math.sqrt(2.0 / math.pi) * (x + 0.044715 * (x * x * x))))


def _softplus(x):
    return jnp.maximum(x, 0.0) + jnp.log(1.0 + jnp.exp(-jnp.abs(x)))


def _modnorm(x, g, scale, shift):
    y = x * lax.rsqrt(jnp.mean(x * x, axis=-1, keepdims=True) + EPS)
    return (y * g) * (1.0 + scale) + shift


def _ada_kernel(c_ref, w_ref, b_ref, o_ref):
    o_ref[0] = _bdot(_silu(c_ref[...]), w_ref[0]) + b_ref[0]


def _ada_mod(c_all, ada_w, ada_b):
    depth, d, n3 = ada_w.shape
    m = c_all.shape[0]
    tn = 1024
    return pl.pallas_call(
        _ada_kernel,
        grid=(depth, n3 // tn),
        in_specs=[pl.BlockSpec((m, d), lambda l, j: (0, 0)),
                  pl.BlockSpec((1, d, tn), lambda l, j: (l, 0, j)),
                  pl.BlockSpec((1, 1, tn), lambda l, j: (l, 0, j))],
        out_specs=pl.BlockSpec((1, m, tn), lambda l, j: (l, 0, j)),
        out_shape=jax.ShapeDtypeStruct((depth, m, n3), F32),
        compiler_params=_cparams("parallel", "parallel"),
        name="ada_mod",
    )(c_all, ada_w, ada_b.reshape(depth, 1, n3))


def _a_branch_pre(p_u, p_v, p_z, vng, vnb):
    u = _gelu(p_u)
    vg = _gelu(p_v)
    vns = []
    for g in range(A_GROUPS):
        seg = vg[:, g * A_GROUP_DIM:(g + 1) * A_GROUP_DIM]
        mu = jnp.mean(seg, axis=-1, keepdims=True)
        xc = seg - mu
        var = jnp.mean(xc * xc, axis=-1, keepdims=True)
        vns.append(xc * lax.rsqrt(var + LN_EPS))
    vn = jnp.concatenate(vns, axis=-1) * vng + vnb
    return u, vn, _silu(p_z)


def _head_sum(x, gmat):
    hi, lo = _split2(x)
    return (jnp.dot(hi, gmat, preferred_element_type=F32)
            + jnp.dot(lo, gmat, preferred_element_type=F32))


def _b_branch_pre(pb, prev, p_bz, mu, w0, w2a, a0, kkw, kaw, rk, gmat):
    xs = pb + mu * (prev - pb)
    r = xs[:, 0:B_WIDTH]
    k = xs[:, B_WIDTH:2 * B_WIDTH]
    v = xs[:, 2 * B_WIDTH:3 * B_WIDTH]
    wa = xs[:, 3 * B_WIDTH:3 * B_WIDTH + 2 * LORA]
    lane = lax.broadcasted_iota(jnp.int32, wa.shape, 1)
    z = jnp.where(lane < LORA, jnp.tanh(wa), wa)
    lora = _bdot(z, w2a)
    w = -_softplus(-(w0 + lora[:, 0:B_WIDTH])) - 0.5
    logd = -jnp.exp(w)
    a = _sigmoid(a0 + lora[:, B_WIDTH:2 * B_WIDTH])
    kk = k * kkw
    kkn = kk / jnp.maximum(jnp.sqrt(_head_sum(kk * kk, gmat)), 1e-12)
    kmod = k * (1.0 + (a - 1.0) * kaw)
    bonus = _head_sum(r * kmod * rk, gmat) * v
    return r, logd, kmod, v, kkn, kkn * a, _silu(p_bz), bonus


def _even_in_kernel(x_ref, scale_ref, shift_ref, g_ref, w_ref, vng_ref, vnb_ref, ws_ref, bst_ref,
                    mu_ref, w0_ref, w2a_ref, a0_ref, kkw_ref, kaw_ref, rk_ref, gmat_ref,
                    outa_ref, r_ref, logd_ref, kmod_ref, v_ref, kkn_ref, beta_ref, gz_ref, bonus_ref,
                    shift_out_ref, carry_ref):
    i = pl.program_id(1)
    tm = x_ref.shape[1]

    @pl.when(i == 0)
    def _():
        carry_ref[...] = jnp.zeros_like(carry_ref)

    h = _modnorm(x_ref[0], g_ref[...], scale_ref[0], shift_ref[0]).astype(BF16)

    def proj(lo, width):
        return jnp.dot(h, w_ref[:, lo:lo + width], preferred_element_type=F32)

    u, vn, gate_a = _a_branch_pre(proj(0, A_WIDTH), proj(A_WIDTH, A_WIDTH), proj(2 * A_WIDTH, A_WIDTH),
                                  vng_ref[...], vnb_ref[...])
    ri = lax.broadcasted_iota(jnp.int32, (CHUNK, CHUNK), 0)
    ci = lax.broadcasted_iota(jnp.int32, (CHUNK, CHUNK), 1)
    vnb16 = vn.astype(BF16)
    cols = []
    for g in range(A_GROUPS):
        wg = jnp.where(ci <= ri, ws_ref[g], 0.0).astype(BF16)
        bias = bst_ref[:, g:g + 1]
        rows = []
        for c in range(tm // CHUNK):
            seg = vnb16[c * CHUNK:(c + 1) * CHUNK, g * A_GROUP_DIM:(g + 1) * A_GROUP_DIM]
            rows.append(jnp.dot(wg, seg, preferred_element_type=F32) + bias)
        cols.append(jnp.concatenate(rows, axis=0))
    mix = jnp.concatenate(cols, axis=-1)
    outa_ref[0] = (u * mix * gate_a).astype(outa_ref.dtype)

    pb = proj(PB_OFF, B_SHIFT)
    row = lax.broadcasted_iota(jnp.int32, pb.shape, 0)
    prev = jnp.where(row == 0, carry_ref[...], pltpu.roll(pb, 1, 0))
    last = pb[tm - 1:tm, :]
    carry_ref[...] = last
    shift_out_ref[0] = last
    outs = _b_branch_pre(pb, prev, proj(BZ_OFF, B_WIDTH), mu_ref[...], w0_ref[...], w2a_ref[...],
                         a0_ref[...], kkw_ref[...], kaw_ref[...], rk_ref[...], gmat_ref[...])
    for ref, val in zip((r_ref, logd_ref, kmod_ref, v_ref, kkn_ref, beta_ref, gz_ref, bonus_ref), outs):
        ref[0] = val


def _row(v):
    return v.reshape(1, -1)


def _even_weights(e, even_w_in, a_vn_g, a_vn_b, a_ws, a_bs, b_mu, b_w0, b_w2, b_a0, b_a2, b_kk, b_ka, b_rk):
    zeros = jnp.zeros((LORA, B_WIDTH), F32)
    w2a = jnp.concatenate([jnp.concatenate([b_w2[e], zeros], axis=1),
                           jnp.concatenate([zeros, b_a2[e]], axis=1)], axis=0).astype(BF16)
    hid = jnp.arange(B_WIDTH) // B_HS
    gmat = (hid[:, None] == hid[None, :]).astype(BF16)
    return dict(w_in=even_w_in[e].astype(BF16), vng=_row(a_vn_g[e]), vnb=_row(a_vn_b[e]), ws=a_ws[e],
                bst=a_bs[e].T, w00=_row(jnp.repeat(a_ws[e][:, 0, 0], A_GROUP_DIM)),
                bs0=_row(jnp.repeat(a_bs[e][:, 0], A_GROUP_DIM)), mu=_row(b_mu[e]), w0=_row(b_w0[e]), w2a=w2a, a0=_row(b_a0[e]),
                kkw=_row(b_kk[e]), kaw=_row(b_ka[e]), rk=_row(b_rk[e]), gmat=gmat)


def _full(shape):
    nd = len(shape)
    return pl.BlockSpec(shape, lambda *_: (0,) * nd)


def _resident(shape):
    nd = len(shape)
    return pl.BlockSpec(shape, lambda *_: (0,) * nd, pipeline_mode=pl.Buffered(1))


def _even_in_prompt(x, scale, shift, norm_g, ew, tm=512):
    bsz, t, d = x.shape
    tok = lambda w: pl.BlockSpec((1, tm, w), lambda b, i: (b, i, 0))
    vec = pl.BlockSpec((1, 1, d), lambda b, i: (b, 0, 0))
    f32out = jax.ShapeDtypeStruct((bsz, t, B_WIDTH), F32)
    return pl.pallas_call(
        _even_in_kernel,
        grid=(bsz, t // tm),
        in_specs=[tok(d), vec, vec, _full((1, d)), _resident((d, EVEN_COLS)), _full((1, A_WIDTH)),
                  _full((1, A_WIDTH)), _full((A_GROUPS, CHUNK, CHUNK)), _full((CHUNK, A_GROUPS)),
                  _full((1, B_SHIFT)), _full((1, B_WIDTH)), _full((2 * LORA, 2 * B_WIDTH)),
                  _full((1, B_WIDTH)), _full((1, B_WIDTH)), _full((1, B_WIDTH)), _full((1, B_WIDTH)),
                  _full((B_WIDTH, B_WIDTH))],
        out_specs=[tok(A_WIDTH)] + [tok(B_WIDTH)] * 8 + [pl.BlockSpec((1, 1, B_SHIFT), lambda b, i: (b, 0, 0))],
        out_shape=[jax.ShapeDtypeStruct((bsz, t, A_WIDTH), BF16)] + [f32out] * 8
                  + [jax.ShapeDtypeStruct((bsz, 1, B_SHIFT), F32)],
        scratch_shapes=[pltpu.VMEM((1, B_SHIFT), F32)],
        compiler_params=_cparams("parallel", "arbitrary"),
        name="even_in_prompt",
    )(x, scale, shift, _row(norm_g), ew["w_in"], ew["vng"], ew["vnb"], ew["ws"], ew["bst"], ew["mu"],
      ew["w0"], ew["w2a"], ew["a0"], ew["kkw"], ew["kaw"], ew["rk"], ew["gmat"])


def _scan_kernel(r_ref, logd_ref, kmod_ref, v_ref, kkn_ref, beta_ref, gz_ref, bonus_ref, gng_ref, gnb_ref,
                 outb_ref, sfin_ref, s_ref):
    ib = pl.program_id(0)
    bsz = r_ref.shape[0]
    n_chunks = r_ref.shape[1] // SCAN_L
    L, L2 = SCAN_L, 2 * SCAN_L

    @pl.when(ib == 0)
    def _():
        s_ref[...] = jnp.zeros_like(s_ref)

    ri = lax.broadcasted_iota(jnp.int32, (L2, L2), 0)
    ci = lax.broadcasted_iota(jnp.int32, (L2, L2), 1)
    same_head = (ri // L) == (ci // L)
    strict = same_head & (ci < ri)
    incl = same_head & (ci <= ri)
    eye = (ri == ci).astype(F32)
    tri = (lax.broadcasted_iota(jnp.int32, (L, L), 1) <= lax.broadcasted_iota(jnp.int32, (L, L), 0))
    tri3 = jnp.concatenate([tri.astype(BF16)] * 3, axis=1)

    def stack(x):
        lane = lax.broadcasted_iota(jnp.int32, x.shape, 1)
        return jnp.concatenate([jnp.where(lane < B_HS, x, 0.0), jnp.where(lane >= B_HS, x, 0.0)], axis=0)

    def dup(x):
        return jnp.concatenate([x, x], axis=0)

    lane = lax.broadcasted_iota(jnp.int32, (L, 2 * B_HS), 1)
    h0 = lane < B_HS
    chains = [(b, j) for b in range(bsz) for j in range(B_PAIRS)]

    def hmean(z):
        m0 = jnp.sum(jnp.where(h0, z, 0.0), axis=-1, keepdims=True)
        m1 = jnp.sum(jnp.where(h0, 0.0, z), axis=-1, keepdims=True)
        return jnp.where(h0, m0, m1) * (1.0 / B_HS)

    def chunk(c, carry):
        sl = pl.ds(pl.multiple_of(c * L, L), L)
        lhs, rhs, v_st, bk, p_last = {}, {}, {}, {}, {}
        for b in range(bsz):
            logd = logd_ref[b, sl, :]
            cs = jnp.dot(tri3, jnp.concatenate(_split3(logd), axis=0), preferred_element_type=F32)
            p = jnp.exp(cs)
            pinv = jnp.exp(-cs)
            to_end = jnp.exp(cs[L - 1:L, :] - cs)
            at = -kkn_ref[b, sl, :] * jnp.exp(cs - logd)
            rt = r_ref[b, sl, :] * p
            beta = beta_ref[b, sl, :]
            kmod = kmod_ref[b, sl, :]
            bt, kt, bl, kl = beta * pinv, kmod * pinv, beta * to_end, kmod * to_end
            vv = v_ref[b, sl, :]
            for j in range(B_PAIRS):
                ln = slice(2 * B_HS * j, 2 * B_HS * (j + 1))
                lhs[b, j] = jnp.concatenate([stack(at[:, ln]), stack(rt[:, ln])], axis=0).astype(BF16)
                rhs[b, j] = jnp.concatenate([dup(bt[:, ln]), dup(kt[:, ln])], axis=0).astype(BF16)
                v_st[b, j] = stack(vv[:, ln]).astype(BF16)
                bk[b, j] = jnp.concatenate([stack(bl[:, ln]), stack(kl[:, ln])], axis=0).astype(BF16)
                p_last[b, j] = p[L - 1:L, ln]
        nt = (((1,), (1,)), ((), ()))
        m1 = {ch: lax.dot_general(lhs[ch], rhs[ch], nt, preferred_element_type=F32) for ch in chains}
        s_old = {(b, j): s_ref[b * B_PAIRS + j] for b, j in chains}
        g0 = {}
        for ch in chains:
            s_hi, s_lo = _split2(s_old[ch])
            g0[ch] = (lax.dot_general(lhs[ch], s_hi, nt, preferred_element_type=F32)
                      + lax.dot_general(lhs[ch], s_lo, nt, preferred_element_type=F32))
        nbd = {ch: jnp.where(strict, m1[ch][0:L2, 0:L2], 0.0) for ch in chains}
        akbd = {ch: jnp.where(strict, m1[ch][0:L2, L2:], 0.0).astype(BF16) for ch in chains}
        rbk = {ch: jnp.concatenate([jnp.where(incl, m1[ch][L2:, 0:L2], 0.0),
                                    jnp.where(incl, m1[ch][L2:, L2:], 0.0)], axis=1).astype(BF16) for ch in chains}
        rhs_u = {ch: g0[ch][0:L2] + jnp.dot(akbd[ch], v_st[ch], preferred_element_type=F32) for ch in chains}
        tmat = {ch: eye + nbd[ch] for ch in chains}
        pw = {ch: _bdot(nbd[ch], nbd[ch]) for ch in chains}
        steps = int(math.log2(L)) - 1
        for k in range(steps):
            for ch in chains:
                if k + 1 < steps:
                    both = _bdot(pw[ch], jnp.concatenate([pw[ch], tmat[ch]], axis=1))
                    tmat[ch] = tmat[ch] + both[:, L2:]
                    pw[ch] = both[:, 0:L2]
                else:
                    tmat[ch] = tmat[ch] + _bdot(pw[ch], tmat[ch])
        u_st = {ch: _bdot(tmat[ch], rhs_u[ch]).astype(BF16) for ch in chains}
        uv = {ch: jnp.concatenate([u_st[ch], v_st[ch]], axis=0) for ch in chains}
        for b, j in chains:
            ch = (b, j)
            ln = slice(2 * B_HS * j, 2 * B_HS * (j + 1))
            y_st = g0[ch][L2:] + jnp.dot(rbk[ch], uv[ch], preferred_element_type=F32)
            y = y_st[0:L] + y_st[L:]
            s_ref[b * B_PAIRS + j] = s_old[ch] * p_last[ch] + lax.dot_general(
                uv[ch], bk[ch], (((0,), (0,)), ((), ())), preferred_element_type=F32)
            yc = y - hmean(y)
            yn = yc * lax.rsqrt(hmean(yc * yc) + GN_EPS) * gng_ref[:, ln] + gnb_ref[:, ln]
            outb_ref[b, sl, ln] = ((yn + bonus_ref[b, sl, ln]) * gz_ref[b, sl, ln]).astype(outb_ref.dtype)
        return carry

    lax.fori_loop(0, n_chunks, chunk, 0)

    @pl.when(ib == pl.num_programs(0) - 1)
    def _():
        for b, j in chains:
            s = s_ref[b * B_PAIRS + j]
            sfin_ref[b, 2 * j] = s[0:B_HS, 0:B_HS]
            sfin_ref[b, 2 * j + 1] = s[B_HS:, B_HS:]


def _scan_prompt(r, logd, kmod, v, kkn, beta, gz, bonus, gn_g, gn_b, tb=128):
    bsz, t, w = r.shape
    tok = pl.BlockSpec((bsz, tb, w), lambda i: (0, i, 0))
    return pl.pallas_call(
        _scan_kernel,
        grid=(t // tb,),
        in_specs=[tok] * 8 + [_full((1, w)), _full((1, w))],
        out_specs=[tok, _full((bsz, B_HEADS, B_HS, B_HS))],
        out_shape=[jax.ShapeDtypeStruct((bsz, t, w), BF16),
                   jax.ShapeDtypeStruct((bsz, B_HEADS, B_HS, B_HS), F32)],
        scratch_shapes=[pltpu.VMEM((bsz * B_PAIRS, 2 * B_HS, 2 * B_HS), F32)],
        compiler_params=_cparams("arbitrary"),
        name="rwkv7_scan_prompt",
    )(r, logd, kmod, v, kkn, beta, gz, bonus, _row(gn_g), _row(gn_b))


def _out_proj_kernel(*refs, n_in, final):
    ins = refs[:n_in]
    w_ref, x_ref, gate_ref = refs[n_in:n_in + 3]
    rest = refs[n_in + 3:]
    acc = None
    off = 0
    for a_ref in ins:
        wd = a_ref.shape[-1]
        part = jnp.dot(a_ref[0], w_ref[off:off + wd, :], preferred_element_type=F32)
        acc = part if acc is None else acc + part
        off += wd
    xn = x_ref[0] + gate_ref[0] * acc
    if final:
        fg_ref, y_ref = rest
        y_ref[0] = xn * lax.rsqrt(jnp.mean(xn * xn, axis=-1, keepdims=True) + EPS) * fg_ref[...]
    else:
        (xo_ref,) = rest
        xo_ref[0] = xn


def _out_proj(acts, w_out, x, gate, final_g=None, tm=512):
    bsz, t, d = x.shape
    tm = min(tm, t)
    final = final_g is not None
    tok = lambda w: pl.BlockSpec((1, tm, w), lambda b, i: (b, i, 0))
    gate_spec = tok(d) if gate.shape[1] == t and t > 1 else pl.BlockSpec((1, 1, d), lambda b, i: (b, 0, 0))
    in_specs = [tok(a.shape[-1]) for a in acts] + [_full(w_out.shape), tok(d), gate_spec]
    args = list(acts) + [w_out, x, gate]
    if final:
        in_specs.append(_full((1, d)))
        args.append(_row(final_g))
    return pl.pallas_call(
        functools.partial(_out_proj_kernel, n_in=len(acts), final=final),
        grid=(bsz, t // tm),
        in_specs=in_specs, out_specs=tok(d), out_shape=jax.ShapeDtypeStruct((bsz, t, d), F32),
        compiler_params=_cparams("parallel", "parallel"),
        name="out_proj",
    )(*args)


def _rope_tables(pos):
    half = ROT_DIM // 2
    inv = ROPE_THETA ** (-jnp.arange(half, dtype=F32) / half)
    ang = pos.astype(F32)[:, None] * inv[None]
    cos, sin = jnp.cos(ang), jnp.sin(ang)
    n = pos.shape[0]
    pad_c = jnp.ones((n, C_HD - ROT_DIM), F32)
    pad_s = jnp.zeros((n, C_HD - ROT_DIM), F32)
    cos_h = jnp.concatenate([cos, cos, pad_c], axis=1)
    sin_h = jnp.concatenate([-sin, sin, pad_s], axis=1)
    return jnp.concatenate([cos_h, cos_h], axis=1), jnp.concatenate([sin_h, sin_h], axis=1)


def _rope(x, cos_t, sin_t):
    half = ROT_DIM // 2
    lane = lax.broadcasted_iota(jnp.int32, x.shape, 1) % C_HD
    partner = jnp.where(lane < half, pltpu.roll(x, 2 * C_HD - half, 1), pltpu.roll(x, half, 1))
    return x * cos_t + partner * sin_t


def _odd_in_kernel(x_ref, scale_ref, shift_ref, g_ref, w_ref, cos_ref, sin_ref, *out_refs, q_scale,
                   for_attention):
    if for_attention:
        k_ref, v_ref, gz_ref, *attn_refs = out_refs
    else:
        q_ref, k_ref, v_ref, gz_ref = out_refs
        attn_refs = ()
    h = _modnorm(x_ref[0], g_ref[...], scale_ref[0], shift_ref[0]).astype(BF16)
    cos_t, sin_t = cos_ref[...], sin_ref[...]
    qk = C_HEADS * 2 * C_HD
    v = jnp.dot(h, w_ref[:, 2 * qk:2 * qk + C_WIDTH], preferred_element_type=F32)
    v_ref[0] = v
    for hd in range(C_HEADS):
        ln = slice(hd * 2 * C_HD, (hd + 1) * 2 * C_HD)
        q = jnp.dot(h, w_ref[:, hd * 2 * C_HD:(hd + 1) * 2 * C_HD], preferred_element_type=F32)
        q = _rope(q, cos_t, sin_t) * q_scale
        k = jnp.dot(h, w_ref[:, qk + hd * 2 * C_HD:qk + (hd + 1) * 2 * C_HD], preferred_element_type=F32)
        k = _rope(k, cos_t, sin_t)
        k_ref[0, :, ln] = k
        if attn_refs:
            qt_ref, kb_ref, vt_ref = attn_refs
            qt_ref[0, ln, :] = q.T.astype(BF16)
            kb_ref[0, :, ln] = k.astype(BF16)
            vt_ref[0, ln, :] = v[:, ln].T.astype(BF16)
        else:
            q_ref[0, :, ln] = q
    gz_ref[0] = _silu(jnp.dot(h, w_ref[:, 2 * qk + C_WIDTH:], preferred_element_type=F32))


def _odd_in(x, scale, shift, norm_g, w_in, cos_t, sin_t, q_scale, for_attention, tm=512):
    bsz, t, d = x.shape
    tm = min(tm, t)
    tok = lambda w: pl.BlockSpec((1, tm, w), lambda b, i: (b, i, 0))
    tokt = pl.BlockSpec((1, C_WIDTH, tm), lambda b, i: (b, 0, i))
    vec = tok(d) if scale.shape[1] == t and t > 1 else pl.BlockSpec((1, 1, d), lambda b, i: (b, 0, 0))
    if cos_t.shape[0] == 1:
        tab = _full((1, 2 * C_HD))
    else:
        tab = pl.BlockSpec((tm, 2 * C_HD), lambda b, i: (i, 0))
    out = lambda dt: jax.ShapeDtypeStruct((bsz, t, C_WIDTH), dt)
    outt = jax.ShapeDtypeStruct((bsz, C_WIDTH, t), BF16)
    out_specs = [tok(C_WIDTH)] * 4
    out_shape = [out(F32)] * 4
    if for_attention:
        out_specs = out_specs[1:] + [tokt, tok(C_WIDTH), tokt]
        out_shape = out_shape[1:] + [outt, out(BF16), outt]
    return pl.pallas_call(
        functools.partial(_odd_in_kernel, q_scale=q_scale, for_attention=for_attention),
        grid=(bsz, t // tm),
        in_specs=[tok(d), vec, vec, _full((1, d)), _resident((d, ODD_COLS)), tab, tab],
        out_specs=out_specs,
        out_shape=out_shape,
        compiler_params=_cparams("parallel", "parallel"),
        name="odd_in",
    )(x, scale, shift, _row(norm_g), w_in, cos_t, sin_t)


def _diff_lambda(lq1, lk1, lq2, lk2, lam_init):
    return (jnp.exp(jnp.sum(lq1 * lk1, axis=-1, keepdims=True))
            - jnp.exp(jnp.sum(lq2 * lk2, axis=-1, keepdims=True)) + lam_init)


def _attn_finish(o1, o2, lam, subln_g, gz, lam_init):
    o = o1 - lam * o2
    o = o * lax.rsqrt(jnp.mean(o * o, axis=-1, keepdims=True) + EPS) * subln_g
    return o * (1.0 - lam_init) * gz


def _attn_kernel(qi_ref, ki_ref, qt_ref, k_ref, vt_ref, gz_ref, lq1_ref, lk1_ref, lq2_ref, lk2_ref, g_ref,
                 o_ref, qm_ref, m_ref, l_ref, acc_ref, *, lam_init):
    pid = pl.program_id(2)
    qi = qi_ref[pid]
    ki = ki_ref[pid]
    tq, tk = qt_ref.shape[2], k_ref.shape[1]

    @pl.when(ki == 0)
    def _():
        qt = qt_ref[0]
        row = lax.broadcasted_iota(jnp.int32, qt.shape, 0)
        qm_ref[0] = jnp.where(row < C_HD, qt, jnp.zeros_like(qt))
        qm_ref[1] = jnp.where(row >= C_HD, qt, jnp.zeros_like(qt))
        m_ref[...] = jnp.full_like(m_ref, -jnp.inf)
        l_ref[...] = jnp.zeros_like(l_ref)
        acc_ref[...] = jnp.zeros_like(acc_ref)

    def step(masked):
        k = k_ref[0]
        vt = vt_ref[0]
        for idx in range(2):
            s = jnp.dot(k, qm_ref[idx], preferred_element_type=F32)
            if masked:
                kpos = ki * tk + lax.broadcasted_iota(jnp.int32, s.shape, 0)
                qpos = qi * tq + lax.broadcasted_iota(jnp.int32, s.shape, 1)
                s = jnp.where(kpos <= qpos, s, -jnp.inf)
            m_old = m_ref[idx]
            m_new = jnp.maximum(m_old, jnp.max(s, axis=0, keepdims=True))
            alpha = jnp.exp2(m_old - m_new)
            p = jnp.exp2(s - m_new)
            l_ref[idx] = alpha * l_ref[idx] + jnp.sum(p, axis=0, keepdims=True)
            acc_ref[idx] = alpha * acc_ref[idx] + jnp.dot(vt, p.astype(BF16), preferred_element_type=F32)
            m_ref[idx] = m_new

    crosses_diagonal = (ki + 1) * tk - 1 > qi * tq

    @pl.when(jnp.logical_not(crosses_diagonal))
    def _():
        step(False)

    @pl.when(crosses_diagonal)
    def _():
        step(True)

    @pl.when(ki == ((qi + 1) * tq - 1) // tk)
    def _():
        lam = _diff_lambda(lq1_ref[...], lk1_ref[...], lq2_ref[...], lk2_ref[...], lam_init)
        o1 = (acc_ref[0] * (1.0 / l_ref[0])).T
        o2 = (acc_ref[1] * (1.0 / l_ref[1])).T
        o_ref[0] = _attn_finish(o1, o2, lam, g_ref[...], gz_ref[0], lam_init).astype(o_ref.dtype)


def _diff_attn_prompt(qt, kb, vt, gz, lq1, lk1, lq2, lk2, subln_g, lam_init, tq=1024, tk=512):
    bsz, t, _ = kb.shape
    tq, tk = min(tq, t), min(tk, t)
    nq = t // tq
    pairs = [(i, j) for i in range(nq) for j in range(((i + 1) * tq - 1) // tk + 1)]
    qi_of = jnp.asarray([p[0] for p in pairs], jnp.int32)
    ki_of = jnp.asarray([p[1] for p in pairs], jnp.int32)
    hw = 2 * C_HD
    qspec = pl.BlockSpec((1, tq, hw), lambda b, h, p, qi, ki: (b, qi[p], h))
    qtspec = pl.BlockSpec((1, hw, tq), lambda b, h, p, qi, ki: (b, h, qi[p]))
    kspec = pl.BlockSpec((1, tk, hw), lambda b, h, p, qi, ki: (b, ki[p], h))
    vtspec = pl.BlockSpec((1, hw, tk), lambda b, h, p, qi, ki: (b, h, ki[p]))
    small = lambda w: pl.BlockSpec((1, w), lambda b, h, p, qi, ki: (0, 0))
    grid_spec = pltpu.PrefetchScalarGridSpec(
        num_scalar_prefetch=2,
        grid=(bsz, C_HEADS, len(pairs)),
        in_specs=[qtspec, kspec, vtspec, qspec, small(C_HD), small(C_HD), small(C_HD), small(C_HD),
                  small(C_VD)],
        out_specs=qspec,
        scratch_shapes=[pltpu.VMEM((2, hw, tq), BF16), pltpu.VMEM((2, 1, tq), F32),
                        pltpu.VMEM((2, 1, tq), F32), pltpu.VMEM((2, C_VD, tq), F32)],
    )
    return pl.pallas_call(
        functools.partial(_attn_kernel, lam_init=lam_init),
        grid_spec=grid_spec,
        out_shape=jax.ShapeDtypeStruct((bsz, t, C_WIDTH), BF16),
        compiler_params=_cparams("parallel", "parallel", "arbitrary"),
        name="diff_attn_prompt",
    )(qi_of, ki_of, qt, kb, vt, gz, _row(lq1), _row(lk1), _row(lq2), _row(lk2), _row(subln_g))


def _even_in_sample_kernel(x_ref, scale_ref, shift_ref, g_ref, w_ref, vng_ref, vnb_ref, w00_ref, bs0_ref,
                           prev_ref, mu_ref, w0_ref, w2a_ref, a0_ref, kkw_ref, kaw_ref, rk_ref, gmat_ref,
                           outa_ref, vn_ref, pb_ref, *t_refs):
    h = _modnorm(x_ref[...], g_ref[...], scale_ref[...], shift_ref[...]).astype(BF16)

    def proj(lo, width):
        return jnp.dot(h, w_ref[:, lo:lo + width], preferred_element_type=F32)

    u, vn, gate_a = _a_branch_pre(proj(0, A_WIDTH), proj(A_WIDTH, A_WIDTH), proj(2 * A_WIDTH, A_WIDTH),
                                  vng_ref[...], vnb_ref[...])
    vn_ref[...] = vn
    mix = vn.astype(BF16).astype(F32) * w00_ref[...].astype(BF16).astype(F32) + bs0_ref[...]
    outa_ref[...] = (u * mix * gate_a).astype(outa_ref.dtype)

    pb = proj(PB_OFF, B_SHIFT)
    pb_ref[...] = pb
    outs = _b_branch_pre(pb, prev_ref[...], proj(BZ_OFF, B_WIDTH), mu_ref[...], w0_ref[...], w2a_ref[...],
                         a0_ref[...], kkw_ref[...], kaw_ref[...], rk_ref[...], gmat_ref[...])
    for ref, val in zip(t_refs, outs):
        ref[...] = val.T


def _even_in_sample(x, scale, shift, norm_g, ew, prev):
    n, d = x.shape
    t_out = jax.ShapeDtypeStruct((B_WIDTH, n), F32)
    args = (x, scale, shift, _row(norm_g), ew["w_in"], ew["vng"], ew["vnb"], ew["w00"], ew["bs0"], prev,
            ew["mu"], ew["w0"], ew["w2a"], ew["a0"], ew["kkw"], ew["kaw"], ew["rk"], ew["gmat"])
    out_shape = [jax.ShapeDtypeStruct((n, A_WIDTH), BF16), jax.ShapeDtypeStruct((n, A_WIDTH), F32),
                 jax.ShapeDtypeStruct((n, B_SHIFT), F32)] + [t_out] * 8
    return pl.pallas_call(
        _even_in_sample_kernel,
        grid=(1,),
        in_specs=[_full(a.shape) for a in args],
        out_specs=[_full(s.shape) for s in out_shape],
        out_shape=out_shape,
        compiler_params=_cparams("arbitrary"),
        name="even_in_sample",
    )(*args)


def _wkv_sample_kernel(s_ref, r_ref, logd_ref, kmod_ref, v_ref, kkn_ref, beta_ref, gz_ref, bonus_ref,
                       gng_ref, gnb_ref, snew_ref, outb_ref, st_ref, y_ref):
    st_ref[...] = s_ref[...].T
    neg_kk = -kkn_ref[...]
    decay = jnp.exp(logd_ref[...])
    beta = beta_ref[...]
    kmod = kmod_ref[...]
    r = r_ref[...]

    def body(vi, carry):
        rows = pl.ds(pl.multiple_of(vi * B_HS, B_HS), B_HS)
        slab = st_ref[rows, :]
        sa = jnp.sum(slab * neg_kk, axis=0, keepdims=True)
        slab = slab * decay + sa * beta + v_ref[pl.ds(vi, 1), :] * kmod
        st_ref[rows, :] = slab
        y_ref[pl.ds(vi, 1), :] = jnp.sum(slab * r, axis=0, keepdims=True)
        return carry

    lax.fori_loop(0, B_HS, body, 0)
    snew_ref[...] = st_ref[...].T
    y = y_ref[...]
    yc = y - jnp.mean(y, axis=0, keepdims=True)
    yn = yc * lax.rsqrt(jnp.mean(yc * yc, axis=0, keepdims=True) + GN_EPS) * gng_ref[...] + gnb_ref[...]
    outb_ref[...] = (yn + bonus_ref[...]) * gz_ref[...]


def _wkv_sample(state, t_ins, gn_g, gn_b):
    n = state.shape[0]
    hs2 = B_HS * B_HS
    sblk = pl.BlockSpec((n, hs2), lambda h: (0, h))
    tblk = pl.BlockSpec((B_HS, n), lambda h: (h, 0))
    cblk = pl.BlockSpec((B_HS, 1), lambda h: (h, 0))
    return pl.pallas_call(
        _wkv_sample_kernel,
        grid=(B_HEADS,),
        in_specs=[sblk] + [tblk] * 8 + [cblk, cblk],
        out_specs=[sblk, tblk],
        out_shape=[jax.ShapeDtypeStruct(state.shape, F32), jax.ShapeDtypeStruct((B_WIDTH, n), F32)],
        scratch_shapes=[pltpu.VMEM((hs2, n), F32), pltpu.VMEM((B_HS, n), F32)],
        compiler_params=_cparams("parallel"),
        name="wkv_sample",
    )(state, *t_ins, gn_g.reshape(-1, 1), gn_b.reshape(-1, 1))


def _out_proj_sample_kernel(outa_ref, outbt_ref, w_ref, x_ref, gate_ref, xo_ref):
    acc = (jnp.dot(outa_ref[...], w_ref[0:A_WIDTH, :], preferred_element_type=F32)
           + jnp.dot(outbt_ref[...].T.astype(BF16), w_ref[A_WIDTH:, :], preferred_element_type=F32))
    xo_ref[...] = x_ref[...] + gate_ref[...] * acc


def _out_proj_sample(outa, outbt, w_out, x, gate):
    args = (outa, outbt, w_out, x, gate)
    return pl.pallas_call(
        _out_proj_sample_kernel,
        grid=(1,),
        in_specs=[_full(a.shape) for a in args],
        out_specs=_full(x.shape),
        out_shape=jax.ShapeDtypeStruct(x.shape, F32),
        compiler_params=_cparams("arbitrary"),
        name="out_proj_sample",
    )(*args)


def _paged_attn_kernel(pt_ref, q_ref, kn_ref, vn_ref, gz_ref, lq1_ref, lk1_ref, lq2_ref, lk2_ref, g_ref,
                       *rest, pages_per_step, lam_init):
    del pt_ref
    k_refs = rest[:pages_per_step]
    v_refs = rest[pages_per_step:2 * pages_per_step]
    o_ref, m_ref, l_ref, acc_ref = rest[2 * pages_per_step:]
    g = pl.program_id(1)

    @pl.when(g == 0)
    def _():
        m_ref[...] = jnp.full_like(m_ref, -jnp.inf)
        l_ref[...] = jnp.zeros_like(l_ref)
        acc_ref[...] = jnp.zeros_like(acc_ref)

    q = q_ref[0]
    hw = 2 * C_HD
    first = lax.broadcasted_iota(jnp.int32, q.shape, 1) < C_HD
    half_sum = ((lax.broadcasted_iota(jnp.int32, (hw, hw), 0) < C_HD)
                == (lax.broadcasted_iota(jnp.int32, (hw, hw), 1) < C_HD)).astype(BF16)

    def swap_halves(x):
        return pltpu.roll(x, C_HD, x.ndim - 1)

    def absorb(kp, vp):
        npos = kp.shape[0]
        prod = (kp * q[None]).reshape(npos * C_HEADS, hw)
        s = jnp.dot(prod.astype(BF16), half_sum, preferred_element_type=F32)
        m_old = m_ref[...]
        m_new = jnp.maximum(m_old, jnp.max(s.reshape(npos, C_HEADS, hw), axis=0))
        alpha = jnp.exp(m_old - m_new)
        p = jnp.exp(s.reshape(npos, C_HEADS, hw) - m_new[None])
        l_ref[...] = alpha * l_ref[...] + jnp.sum(p, axis=0)
        p_sw = swap_halves(p.reshape(npos * C_HEADS, hw)).reshape(npos, C_HEADS, hw)
        a_sw = swap_halves(alpha)
        acc_ref[0] = (jnp.where(first, alpha, a_sw) * acc_ref[0]
                      + jnp.sum(jnp.where(first[None], p, p_sw) * vp, axis=0))
        acc_ref[1] = (jnp.where(first, a_sw, alpha) * acc_ref[1]
                      + jnp.sum(jnp.where(first[None], p_sw, p) * vp, axis=0))
        m_ref[...] = m_new

    for j in range(pages_per_step):
        absorb(k_refs[j][0, 0], v_refs[j][0, 0])

    @pl.when(g == pl.num_programs(1) - 1)
    def _():
        absorb(kn_ref[...], vn_ref[...])
        lam = _diff_lambda(lq1_ref[...], lk1_ref[...], lq2_ref[...], lk2_ref[...], lam_init)
        l = l_ref[...]
        l_sw = swap_halves(l)
        out = _attn_finish(acc_ref[0] / jnp.where(first, l, l_sw), acc_ref[1] / jnp.where(first, l_sw, l),
                           lam, g_ref[...], gz_ref[0], lam_init)
        o_ref[0] = out.astype(o_ref.dtype)


def _paged_attn_sample(layer, page_table, q, k_new, v_new, gz, cache_k, cache_v, lq1, lk1, lq2, lk2, subln_g,
                       lam_init, pages_per_step=8):
    n, n_pages = page_table.shape
    steps = n_pages // pages_per_step
    tokblk = pl.BlockSpec((1, C_HEADS, 2 * C_HD), lambda b, g, pt: (b, 0, 0))
    small = lambda w: pl.BlockSpec((1, w), lambda b, g, pt: (0, 0))

    def page(j):
        return pl.BlockSpec((1, 1, PAGE_SIZE, C_HEADS, 2 * C_HD),
                            lambda b, g, pt: (layer, pt[b * n_pages + g * pages_per_step + j], 0, 0, 0))

    grid_spec = pltpu.PrefetchScalarGridSpec(
        num_scalar_prefetch=1,
        grid=(n, steps),
        in_specs=[tokblk] * 4 + [small(C_HD)] * 4 + [small(C_VD)]
                 + [page(j) for j in range(pages_per_step)] * 2,
        out_specs=tokblk,
        scratch_shapes=[pltpu.VMEM((C_HEADS, 2 * C_HD), F32), pltpu.VMEM((C_HEADS, 2 * C_HD), F32),
                        pltpu.VMEM((2, C_HEADS, C_VD), F32)],
    )
    return pl.pallas_call(
        functools.partial(_paged_attn_kernel, pages_per_step=pages_per_step, lam_init=lam_init),
        grid_spec=grid_spec,
        out_shape=jax.ShapeDtypeStruct((n, C_HEADS, C_VD), BF16),
        compiler_params=_cparams("parallel", "arbitrary"),
        name="paged_attn_sample",
    )(page_table.reshape(-1), q, k_new, v_new, gz, _row(lq1), _row(lk1), _row(lq2), _row(lk2), _row(subln_g),
      *([cache_k] * pages_per_step), *([cache_v] * pages_per_step))


def kernel(x_prompt, x_sample, c_prompt, c_sample, state_wkv, state_shift, cache_k, cache_v, page_table, ada_w, ada_b, norm_g, final_g, even_w_in, even_w_out, a_vn_g, a_vn_b, a_ws, a_bs, b_mu, b_w0, b_w2, b_a0, b_a2, b_kk, b_ka, b_rk, b_gn_g, b_gn_b, odd_w_in, odd_w_out, c_lq1, c_lk1, c_lq2, c_lk2, c_subln_g):
    depth = ada_w.shape[0]
    assert depth % 2 == 0, "the final RMSNorm is fused into the last (odd) layer's output projection"
    bsz_p, t_p, d = x_prompt.shape
    bsz_s = x_sample.shape[0]
    past = page_table.shape[1] * PAGE_SIZE

    rows = bsz_p + bsz_s
    pad = -rows % 8
    c_all = jnp.concatenate([c_prompt, c_sample, jnp.zeros((pad, d), F32)], axis=0)
    mod = _ada_mod(c_all, ada_w, ada_b)
    shift_p, scale_p, gate_p = [m[:, :bsz_p, None, :] for m in jnp.split(mod, 3, axis=-1)]
    shift_s, scale_s, gate_s = [m[:, bsz_p:rows, :] for m in jnp.split(mod, 3, axis=-1)]

    cos_p, sin_p = _rope_tables(jnp.arange(t_p))
    cos_s, sin_s = _rope_tables(past + jnp.arange(1))

    xp = x_prompt
    xs = x_sample.reshape(bsz_s, d)
    wkv_p, shp_p, k_p, v_p = [], [], [], []
    wkv_s, shp_s, av_s, k_s, v_s = [], [], [], [], []
    y_prompt = y_sample = None
    for l in range(depth):
        last = l == depth - 1
        fg = final_g if last else None
        if l % 2 == 0:
            e = l // 2
            ew = _even_weights(e, even_w_in, a_vn_g, a_vn_b, a_ws, a_bs, b_mu, b_w0, b_w2, b_a0, b_a2, b_kk,
                               b_ka, b_rk)
            w_out = even_w_out[e].astype(BF16)
            outs = _even_in_prompt(xp, scale_p[l], shift_p[l], norm_g[l], ew)
            outb, sfin = _scan_prompt(*outs[1:9], b_gn_g[e], b_gn_b[e])
            wkv_p.append(sfin)
            shp_p.append(outs[9][:, 0])
            xp = _out_proj([outs[0], outb], w_out, xp, gate_p[l], fg)
            souts = _even_in_sample(xs, scale_s[l], shift_s[l], norm_g[l], ew, state_shift[e])
            av_s.append(souts[1][:, None, :])
            shp_s.append(souts[2])
            snew, outbt = _wkv_sample(state_wkv[e].reshape(bsz_s, -1), souts[3:], b_gn_g[e], b_gn_b[e])
            wkv_s.append(snew.reshape(bsz_s, B_HEADS, B_HS, B_HS))
            xs = _out_proj_sample(souts[0], outbt, w_out, xs, gate_s[l])
        else:
            o = l // 2
            lam_init = 0.8 - 0.6 * math.exp(-0.3 * l)
            w_in = odd_w_in[o].astype(BF16)
            w_out = odd_w_out[o].astype(BF16)
            lam_args = (c_lq1[o], c_lk1[o], c_lq2[o], c_lk2[o], c_subln_g[o], lam_init)
            k, v, gz, qt, kb, vt = _odd_in(xp, scale_p[l], shift_p[l], norm_g[l], w_in, cos_p, sin_p,
                                           C_HD ** -0.5 * math.log2(math.e), True)
            k_p.append(k.reshape(bsz_p, t_p, C_HEADS, 2 * C_HD))
            v_p.append(v.reshape(bsz_p, t_p, C_HEADS, C_VD))
            att = _diff_attn_prompt(qt, kb, vt, gz, *lam_args)
            xp = _out_proj([att], w_out, xp, gate_p[l], fg)
            q, k, v, gz = _odd_in(xs[None], scale_s[l][None], shift_s[l][None], norm_g[l], w_in,
                                  cos_s, sin_s, C_HD ** -0.5, False)
            hd = lambda z: z.reshape(bsz_s, C_HEADS, 2 * C_HD)
            k_s.append(k.reshape(bsz_s, 1, C_HEADS, 2 * C_HD))
            v_s.append(v.reshape(bsz_s, 1, C_HEADS, C_VD))
            att = _paged_attn_sample(o, page_table, hd(q), hd(k), hd(v), hd(gz), cache_k, cache_v, *lam_args)
            xs = _out_proj([att.reshape(1, bsz_s, C_WIDTH)], w_out, xs[None], gate_s[l][None], fg)[0]
            if last:
                y_prompt, y_sample = xp, xs
    return (y_prompt, y_sample.reshape(bsz_s, 1, d), jnp.stack(wkv_p), jnp.stack(shp_p), jnp.stack(k_p),
            jnp.stack(v_p), jnp.stack(wkv_s), jnp.stack(shp_s), jnp.stack(av_s), jnp.stack(k_s), jnp.stack(v_s))
```
